```python
import math
import jax, jax.numpy as jnp
from jax import lax
import numpy as np

D_MODEL = 1024
BATCH = 16
SEQ = 2048
DEPTH = 1

PLE_DIM = 256
RMS_EPS = 1e-6
N_BRANCH = 2
S5_WIDTH = 512
S5_GROUP = 16
S5_GROUPS = S5_WIDTH // S5_GROUP
S5_STATE = 64
SSD_WIDTH = 1536
SSD_HEADDIM = 64
SSD_HEADS = SSD_WIDTH // SSD_HEADDIM
SSD_GROUPS = 4
SSD_HPG = SSD_HEADS // SSD_GROUPS
SSD_STATE = 128
SSD_CONV = 4
SSD_CHUNK = 128
SSD_BC = SSD_GROUPS * SSD_STATE
SSD_CONV_DIM = SSD_WIDTH + 2 * SSD_BC
DT_MIN, DT_MAX = 1e-3, 1e-1
_SIZES = (S5_WIDTH, S5_WIDTH, SSD_WIDTH, SSD_CONV_DIM, SSD_HEADS, N_BRANCH * D_MODEL)
IN_PROJ_DIM = int(sum(_SIZES))
SPLITS = tuple(int(v) for v in np.cumsum(_SIZES)[:-1])

kernel_name = "hybrid_s5_ssd_gated_block"


def rms_norm(x, w):
    xf = x.astype(jnp.float32)
    y = xf * lax.rsqrt(jnp.mean(xf * xf, axis=-1, keepdims=True) + RMS_EPS)
    return (y * w.astype(jnp.float32)).astype(x.dtype)


def s5_mixer(u, a_re, a_im, b_re, b_im, c_re, c_im, d, log_step, w_glu, b_glu):
    f32 = jnp.float32
    bsz, L, _ = u.shape
    uf = u.astype(f32).reshape(bsz, L, S5_GROUPS, S5_GROUP)
    a_re = a_re.astype(f32); a_im = a_im.astype(f32)
    step = jnp.exp(log_step.astype(f32))[:, None]
    mag = jnp.exp(a_re * step)
    lb_re = mag * jnp.cos(a_im * step)
    lb_im = mag * jnp.sin(a_im * step)
    den = a_re * a_re + a_im * a_im
    n_re = lb_re - 1.0
    n_im = lb_im
    f_re = (n_re * a_re + n_im * a_im) / den
    f_im = (n_im * a_re - n_re * a_im) / den
    b_re = b_re.astype(f32); b_im = b_im.astype(f32)
    bb_re = f_re[..., None] * b_re - f_im[..., None] * b_im
    bb_im = f_re[..., None] * b_im + f_im[..., None] * b_re
    bu_re = jnp.einsum('gph,blgh->blgp', bb_re, uf)
    bu_im = jnp.einsum('gph,blgh->blgp', bb_im, uf)
    ar = jnp.broadcast_to(lb_re, (1, L, S5_GROUPS, S5_STATE))
    ai = jnp.broadcast_to(lb_im, (1, L, S5_GROUPS, S5_STATE))

    def combine(e1, e2):
        a1r, a1i, b1r, b1i = e1
        a2r, a2i, b2r, b2i = e2
        return (a2r * a1r - a2i * a1i,
                a2r * a1i + a2i * a1r,
                a2r * b1r - a2i * b1i + b2r,
                a2r * b1i + a2i * b1r + b2i)

    _, _, s_re, s_im = lax.associative_scan(combine, (ar, ai, bu_re, bu_im), axis=1)
    y = (jnp.einsum('ghp,blgp->blgh', c_re.astype(f32), s_re)
         - jnp.einsum('ghp,blgp->blgh', c_im.astype(f32), s_im))
    y = y.reshape(bsz, L, S5_WIDTH) + d.astype(f32) * u.astype(f32)
    y = jax.nn.gelu(y)
    y = y * jax.nn.sigmoid(y @ w_glu.astype(f32) + b_glu.astype(f32))
    return y.astype(u.dtype)


def causal_depthwise_conv(x, w, b):
    C = x.shape[-1]
    y = lax.conv_general_dilated(x, w[:, None, :], window_strides=(1,),
                                 padding=[(SSD_CONV - 1, 0)],
                                 dimension_numbers=('NWC', 'WIO', 'NWC'),
                                 feature_group_count=C)
    return y + b


def ssd_mixer(z, xbc, dt_raw, conv_w, conv_b, dt_bias, a_log, d_skip, norm_w):
    f32 = jnp.float32
    bsz, L, _ = xbc.shape
    nc = L // SSD_CHUNK
    G, J, P, N, CH = SSD_GROUPS, SSD_HPG, SSD_HEADDIM, SSD_STATE, SSD_CHUNK
    xbc = jax.nn.silu(causal_depthwise_conv(xbc.astype(f32), conv_w.astype(f32), conv_b.astype(f32)))
    xs, bm, cm = jnp.split(xbc, [SSD_WIDTH, SSD_WIDTH + SSD_BC], axis=-1)
    dt = jax.nn.softplus(dt_raw.astype(f32) + dt_bias.astype(f32))
    a = -jnp.exp(a_log.astype(f32))
    xs = xs.reshape(bsz, nc, CH, G, J, P)
    bm = bm.reshape(bsz, nc, CH, G, N)
    cm = cm.reshape(bsz, nc, CH, G, N)
    dtc = dt.reshape(bsz, nc, CH, G, J)
    xdt = xs * dtc[..., None]
    da = jnp.transpose((dt * a).reshape(bsz, nc, CH, G, J), (0, 1, 3, 4, 2))
    a_cum = jnp.cumsum(da, axis=-1)
    seg = a_cum[..., :, None] - a_cum[..., None, :]
    causal = jnp.tril(jnp.ones((CH, CH), dtype=bool))
    lmat = jnp.exp(jnp.where(causal, seg, -jnp.inf))
    scores = jnp.einsum('bclgn,bcsgn->bcgls', cm, bm)
    wts = scores[:, :, :, None] * lmat
    y_diag = jnp.einsum('bcgjls,bcsgjp->bclgjp', wts, xdt)
    decay_states = jnp.exp(a_cum[..., -1:] - a_cum)
    states = jnp.einsum('bclgn,bcgjl,bclgjp->bcgjpn', bm, decay_states, xdt)
    chunk_decay = jnp.exp(a_cum[..., -1])

    def step(carry, inp):
        dec, st = inp
        return carry * dec[..., None, None] + st, carry

    init = jnp.zeros((bsz, G, J, P, N), f32)
    _, prev = lax.scan(step, init, (jnp.moveaxis(chunk_decay, 1, 0), jnp.moveaxis(states, 1, 0)))
    prev = jnp.moveaxis(prev, 0, 1)
    y_off = jnp.einsum('bclgn,bcgjpn,bcgjl->bclgjp', cm, prev, jnp.exp(a_cum))
    y = y_diag + y_off + xs * d_skip.astype(f32).reshape(G, J)[:, :, None]
    y = y.reshape(bsz, L, SSD_WIDTH)
    yg = (y * jax.nn.silu(z.astype(f32))).reshape(bsz, L, G, SSD_WIDTH // G)
    yg = yg * lax.rsqrt(jnp.mean(yg * yg, axis=-1, keepdims=True) + RMS_EPS)
    y = yg.reshape(bsz, L, SSD_WIDTH) * norm_w.astype(f32)
    return y.astype(z.dtype)


def setup_inputs(seed: int = 0) -> dict:
    key = jax.random.key(seed)
    ks = iter(jax.random.split(key, 40))
    nrm = lambda shape, s: jax.random.normal(next(ks), shape, jnp.float32) * s
    D = D_MODEL
    x = jax.random.normal(next(ks), (BATCH, SEQ, D), jnp.float32)
    p = jax.random.normal(next(ks), (DEPTH, BATCH, SEQ, PLE_DIM), jnp.float32)
    norm_w = 1.0 + nrm((DEPTH, D), 0.02)
    w_in = nrm((DEPTH, D, IN_PROJ_DIM), D ** -0.5)
    n_idx = jnp.arange(S5_STATE, dtype=jnp.float32)
    s5_a_re = -0.5 + nrm((DEPTH, S5_GROUPS, S5_STATE), 0.01)
    s5_a_im = math.pi * n_idx + nrm((DEPTH, S5_GROUPS, S5_STATE), 0.01)
    s5_b_re = nrm((DEPTH, S5_GROUPS, S5_STATE, S5_GROUP), (2 * S5_GROUP) ** -0.5)
    s5_b_im = nrm((DEPTH, S5_GROUPS, S5_STATE, S5_GROUP), (2 * S5_GROUP) ** -0.5)
    s5_c_re = nrm((DEPTH, S5_GROUPS, S5_GROUP, S5_STATE), S5_STATE ** -0.5)
    s5_c_im = nrm((DEPTH, S5_GROUPS, S5_GROUP, S5_STATE), S5_STATE ** -0.5)
    s5_d = nrm((DEPTH, S5_WIDTH), 1.0)
    s5_log_step = jax.random.uniform(next(ks), (DEPTH, S5_GROUPS), jnp.float32,
                                     math.log(DT_MIN), math.log(DT_MAX))
    s5_w_glu = nrm((DEPTH, S5_WIDTH, S5_WIDTH), S5_WIDTH ** -0.5)
    s5_b_glu = nrm((DEPTH, S5_WIDTH), 0.01)
    ssd_conv_w = nrm((DEPTH, SSD_CONV, SSD_CONV_DIM), SSD_CONV ** -0.5)
    ssd_conv_b = nrm((DEPTH, SSD_CONV_DIM), 0.01)
    dt0 = jnp.exp(jax.random.uniform(next(ks), (DEPTH, SSD_HEADS), jnp.float32,
                                     math.log(DT_MIN), math.log(DT_MAX)))
    ssd_dt_bias = dt0 + jnp.log(-jnp.expm1(-dt0))
    ssd_a_log = jnp.log(jax.random.uniform(next(ks), (DEPTH, SSD_HEADS), jnp.float32, 1.0, 16.0))
    ssd_d = 1.0 + nrm((DEPTH, SSD_HEADS), 0.1)
    ssd_norm_w = 1.0 + nrm((DEPTH, SSD_WIDTH), 0.02)
    w_br_s5 = nrm((DEPTH, S5_WIDTH, D), S5_WIDTH ** -0.5)
    w_br_ssd = nrm((DEPTH, SSD_WIDTH, D), SSD_WIDTH ** -0.5)
    w_out = nrm((DEPTH, D, D), D ** -0.5)
    ple_norm_w = 1.0 + nrm((DEPTH, D), 0.02)
    w_ple_gate = nrm((DEPTH, D, D), D ** -0.5)
    w_ple_proj = nrm((DEPTH, PLE_DIM, D), PLE_DIM ** -0.5)
    final_norm_w = 1.0 + nrm((D,), 0.02)
    return {"x": x, "p": p, "norm_w": norm_w, "w_in": w_in,
            "s5_a_re": s5_a_re, "s5_a_im": s5_a_im, "s5_b_re": s5_b_re, "s5_b_im": s5_b_im,
            "s5_c_re": s5_c_re, "s5_c_im": s5_c_im, "s5_d": s5_d, "s5_log_step": s5_log_step,
            "s5_w_glu": s5_w_glu, "s5_b_glu": s5_b_glu,
            "ssd_conv_w": ssd_conv_w, "ssd_conv_b": ssd_conv_b, "ssd_dt_bias": ssd_dt_bias,
            "ssd_a_log": ssd_a_log, "ssd_d": ssd_d, "ssd_norm_w": ssd_norm_w,
            "w_br_s5": w_br_s5, "w_br_ssd": w_br_ssd, "w_out": w_out,
            "ple_norm_w": ple_norm_w, "w_ple_gate": w_ple_gate, "w_ple_proj": w_ple_proj,
            "final_norm_w": final_norm_w}


def reference(x, p, norm_w, w_in, s5_a_re, s5_a_im, s5_b_re, s5_b_im, s5_c_re, s5_c_im,
              s5_d, s5_log_step, s5_w_glu, s5_b_glu, ssd_conv_w, ssd_conv_b, ssd_dt_bias,
              ssd_a_log, ssd_d, ssd_norm_w, w_br_s5, w_br_ssd, w_out, ple_norm_w,
              w_ple_gate, w_ple_proj, final_norm_w):
    h = x
    for i in range(DEPTH):
        hn = rms_norm(h, norm_w[i])
        proj = hn @ w_in[i]
        s5_u, s5_z, ssd_z, ssd_xbc, ssd_dt, gate_logits = jnp.split(proj, SPLITS, axis=-1)
        y5 = s5_mixer(s5_u, s5_a_re[i], s5_a_im[i], s5_b_re[i], s5_b_im[i], s5_c_re[i],
                      s5_c_im[i], s5_d[i], s5_log_step[i], s5_w_glu[i], s5_b_glu[i])
        y5 = y5 * jax.nn.silu(s5_z)
        yss = ssd_mixer(ssd_z, ssd_xbc, ssd_dt, ssd_conv_w[i], ssd_conv_b[i], ssd_dt_bias[i],
                        ssd_a_log[i], ssd_d[i], ssd_norm_w[i])
        g5, gss = jnp.split(jax.nn.sigmoid(gate_logits), N_BRANCH, axis=-1)
        merged = g5 * (y5 @ w_br_s5[i]) + gss * (yss @ w_br_ssd[i])
        h = h + merged @ w_out[i]
        ple_gate = jax.nn.sigmoid(rms_norm(h, ple_norm_w[i]) @ w_ple_gate[i])
        h = h + ple_gate * (p[i] @ w_ple_proj[i])
    return rms_norm(h, final_norm_w)
```

```python
import functools
import math

import jax
import jax.numpy as jnp
from jax import lax
from jax.experimental import pallas as pl
from jax.experimental.pallas import tpu as pltpu

F32 = jnp.float32
BF16 = jnp.bfloat16

D_MODEL = 1024
PLE_DIM = 256
RMS_EPS = 1e-6
S5_WIDTH = 512
S5_GROUP = 16
S5_GROUPS = S5_WIDTH // S5_GROUP
S5_STATE = 64
S5_NSTATE = S5_GROUPS * S5_STATE
S5_HALVES = 2
S5_HCH = S5_WIDTH // S5_HALVES
S5_HST = S5_NSTATE // S5_HALVES
SSD_WIDTH = 1536
SSD_HEADDIM = 64
SSD_HEADS = SSD_WIDTH // SSD_HEADDIM
SSD_GROUPS = 4
SSD_HPG = SSD_HEADS // SSD_GROUPS
SSD_STATE = 128
SSD_CONV = 4
SSD_BC = SSD_GROUPS * SSD_STATE
SSD_CONV_DIM = SSD_WIDTH + 2 * SSD_BC
SSD_GW = SSD_WIDTH // SSD_GROUPS
N_BRANCH = 2

LANES = 128
SUBLANES = 8
VMEM_LIMIT_BYTES = 56 * 1024 * 1024

Q = 128
TL_IN = 256
TT_S5 = 32
S5_CW = 256
TL_OUT = 512


def _sigmoid(v):
    return jax.nn.sigmoid(v)


def _silu(v):
    return v * jax.nn.sigmoid(v)


def _softplus(v):
    return jnp.maximum(v, 0.0) + jnp.log1p(jnp.exp(-jnp.abs(v)))


def _rms(v, w):
    return v * lax.rsqrt(jnp.mean(v * v, axis=-1, keepdims=True) + RMS_EPS) * w


def _dot(a, b):
    return jnp.dot(a, b, preferred_element_type=F32)


def _s5_disc_kernel(are_ref, aim_ref, lstep_ref, bre_ref, bim_ref, cim_ref,
                    lre_ref, lim_ref, bbre_ref, bbim_ref, ncim_ref):
    a_re = are_ref[...]
    a_im = aim_ref[...]
    step = jnp.exp(lstep_ref[...])
    mag = jnp.exp(a_re * step)
    lb_re = mag * jnp.cos(a_im * step)
    lb_im = mag * jnp.sin(a_im * step)
    den = a_re * a_re + a_im * a_im
    n_re = lb_re - 1.0
    n_im = lb_im
    f_re = (n_re * a_re + n_im * a_im) / den
    f_im = (n_im * a_re - n_re * a_im) / den
    b_re = bre_ref[...]
    b_im = bim_ref[...]
    lre_ref[...] = lb_re
    lim_ref[...] = lb_im
    bbre_ref[...] = f_re * b_re - f_im * b_im
    bbim_ref[...] = f_re * b_im + f_im * b_re
    ncim_ref[...] = -cim_ref[...]


def _s5_discretize(a_re, a_im, log_step, b_re, b_im, c_im):
    n = S5_NSTATE
    row = lambda v: v.reshape(1, n)
    b_t = lambda v: jnp.transpose(v, (2, 0, 1)).reshape(S5_GROUP, n)
    c_t = lambda v: jnp.transpose(v, (1, 0, 2)).reshape(S5_GROUP, n)
    lstep = jnp.repeat(log_step, S5_STATE).reshape(1, n)
    outs = pl.pallas_call(
        _s5_disc_kernel,
        out_shape=[jax.ShapeDtypeStruct((1, n), F32), jax.ShapeDtypeStruct((1, n), F32),
                   jax.ShapeDtypeStruct((S5_GROUP, n), F32), jax.ShapeDtypeStruct((S5_GROUP, n), F32),
                   jax.ShapeDtypeStruct((S5_GROUP, n), F32)],
        name="s5_discretize",
    )(row(a_re), row(a_im), lstep, b_t(b_re), b_t(b_im), c_t(c_im))
    return outs


def _s5_block_diag(m_t):
    gl = S5_GROUPS // S5_HALVES
    m4 = m_t.reshape(S5_GROUP, S5_HALVES, gl, S5_STATE)
    eye = jnp.eye(gl, dtype=m_t.dtype)
    out = jnp.einsum('hqgp,gk->qghkp', m4, eye)
    return out.reshape(S5_HALVES, S5_HCH, S5_HST)


def _mixer_in_kernel(x_ref, nw_ref, wu_ref, wz5_ref, wzs_ref, wxbc_ref, wdt_ref, wg_ref,
                     convw_ref, convb_ref, dtb_ref, alog_ref, dskip_ref, snw_ref, wbs_ref,
                     u_ref, zs5_ref, g5_ref, ms_ref,
                     ext_ref, xc_ref, y_ref, state_ref):
    tl = x_ref.shape[0]
    tail = SUBLANES

    @pl.when(pl.program_id(1) == 0)
    def _():
        ext_ref[0:tail, :] = jnp.zeros((tail, SSD_CONV_DIM), F32)
        state_ref[...] = jnp.zeros(state_ref.shape, F32)

    hb = _rms(x_ref[...], nw_ref[...]).astype(BF16)
    u_ref[...] = _dot(hb, wu_ref[...])
    zs5_ref[...] = _silu(_dot(hb, wz5_ref[...])).astype(BF16)
    gates = _sigmoid(_dot(hb, wg_ref[...]))
    g5_ref[...] = gates[:, :D_MODEL].astype(BF16)
    ext_ref[tail:tail + tl, :] = _dot(hb, wxbc_ref[...])
    dt_all = _softplus(_dot(hb, wdt_ref[...]) + dtb_ref[...])
    zs = _dot(hb, wzs_ref[...])

    cblk = 512
    for c0 in range(0, SSD_CONV_DIM, cblk):
        cs = slice(c0, c0 + cblk)
        acc = convb_ref[:, cs]
        for k in range(SSD_CONV):
            off = tail - (SSD_CONV - 1) + k
            acc = acc + convw_ref[k:k + 1, cs] * ext_ref[off:off + tl, cs]
        xc_ref[:, cs] = _silu(acc)
    ext_ref[0:tail, :] = ext_ref[tl:tl + tail, :]

    a_row = -jnp.exp(alog_ref[...])
    ri = lax.broadcasted_iota(jnp.int32, (Q, Q), 0)
    ci = lax.broadcasted_iota(jnp.int32, (Q, Q), 1)
    causal = ci <= ri
    tril = causal.astype(F32)
    lane_lo = lax.broadcasted_iota(jnp.int32, (Q, LANES), 1) < SSD_HEADDIM
    b_off = SSD_WIDTH
    c_off = SSD_WIDTH + SSD_BC

    for c in range(tl // Q):
        rows = slice(c * Q, (c + 1) * Q)
        dt = dt_all[rows]
        a_cum = jnp.dot(tril, dt * a_row, preferred_element_type=F32, precision=lax.Precision.HIGHEST)
        a_cum_t = a_cum.T
        dt_t = dt.T
        for g in range(SSD_GROUPS):
            bg = xc_ref[rows, b_off + g * SSD_STATE:b_off + (g + 1) * SSD_STATE].astype(BF16)
            cg = xc_ref[rows, c_off + g * SSD_STATE:c_off + (g + 1) * SSD_STATE].astype(BF16)
            scores = lax.dot_general(cg, bg, (((1,), (1,)), ((), ())), preferred_element_type=F32)
            gcols = slice(g * SSD_GW, (g + 1) * SSD_GW)
            y_off = _dot(cg, state_ref[:, gcols].astype(BF16))
            xw_parts = []
            decay_parts = []
            for k in range(SSD_HPG // 2):
                cols = slice(g * SSD_GW + k * LANES, g * SSD_GW + (k + 1) * LANES)
                xs = xc_ref[rows, cols]
                wts, ea, wcol = [], [], []
                for h in (g * SSD_HPG + 2 * k, g * SSD_HPG + 2 * k + 1):
                    a_col = jnp.broadcast_to(a_cum[:, h:h + 1], (Q, Q))
                    seg = a_col - a_cum_t[h:h + 1, :]
                    lmat = jnp.exp(jnp.where(causal, seg, -jnp.inf))
                    wts.append((scores * lmat * dt_t[h:h + 1, :]).astype(BF16))
                    ea.append(jnp.exp(a_col))
                    dt_col = jnp.broadcast_to(dt[:, h:h + 1], (Q, LANES))
                    wcol.append(jnp.exp(a_cum[Q - 1:Q, h:h + 1] - a_col) * dt_col)
                xb = xs.astype(BF16)
                zero = jnp.zeros_like(xb)
                rhs = jnp.concatenate([jnp.where(lane_lo, xb, zero), jnp.where(lane_lo, zero, xb)], axis=0)
                y_diag = _dot(jnp.concatenate(wts, axis=1), rhs)
                ea2 = jnp.where(lane_lo, ea[0], ea[1])
                wcol2 = jnp.where(lane_lo, wcol[0], wcol[1])
                y_ref[rows, cols] = (y_diag + y_off[:, k * LANES:(k + 1) * LANES] * ea2
                                     + xs * dskip_ref[:, cols])
                xw_parts.append((xs * wcol2).astype(BF16))
                decay_parts.append(ea2[Q - 1:Q, :])
            xw = jnp.concatenate(xw_parts, axis=1)
            new = lax.dot_general(bg, xw, (((0,), (0,)), ((), ())), preferred_element_type=F32)
            decay = jnp.concatenate(decay_parts, axis=1)
            state_ref[:, gcols] = state_ref[:, gcols] * decay + new

    yz = y_ref[...] * _silu(zs)
    parts = []
    for g in range(SSD_GROUPS):
        gcols = slice(g * SSD_GW, (g + 1) * SSD_GW)
        yg = yz[:, gcols]
        parts.append(_rms(yg, snw_ref[:, gcols]).astype(BF16))
    yn = jnp.concatenate(parts, axis=1)
    ms_ref[...] = (gates[:, D_MODEL:] * _dot(yn, wbs_ref[...])).astype(BF16)


def _const_spec(shape):
    zeros = (0,) * len(shape)
    return pl.BlockSpec(shape, lambda *_: zeros)


def _mixer_in(x, norm_w, w_in, conv_w, conv_b, dt_bias, a_log, d_skip, ssd_norm_w, w_br_ssd):
    bsz, seq, d = x.shape
    tl = TL_IN
    sizes = (S5_WIDTH, S5_WIDTH, SSD_WIDTH, SSD_CONV_DIM, SSD_HEADS, N_BRANCH * D_MODEL)
    offs = [0]
    for s in sizes:
        offs.append(offs[-1] + s)
    wb = w_in.astype(BF16)
    wu, wz5, wzs, wxbc, wdt, wg = (wb[:, offs[i]:offs[i + 1]] for i in range(6))
    pad = LANES - SSD_HEADS
    wdt = jnp.pad(wdt, ((0, 0), (0, pad)))
    dtb = jnp.pad(dt_bias.reshape(1, SSD_HEADS), ((0, 0), (0, pad)))
    alog = jnp.pad(a_log.reshape(1, SSD_HEADS), ((0, 0), (0, pad)))
    dskip = jnp.repeat(d_skip, SSD_HEADDIM).reshape(1, SSD_WIDTH)
    consts = [norm_w.reshape(1, d), wu, wz5, wzs, wxbc, wdt, wg, conv_w, conv_b.reshape(1, -1), dtb, alog,
              dskip, ssd_norm_w.reshape(1, -1), w_br_ssd.astype(BF16)]
    tok_spec = lambda w: pl.BlockSpec((tl, w), lambda b, t: (b * (seq // tl) + t, 0))
    return pl.pallas_call(
        _mixer_in_kernel,
        grid=(bsz, seq // tl),
        in_specs=[pl.BlockSpec((None, tl, d), lambda b, t: (b, t, 0))] + [_const_spec(c.shape) for c in consts],
        out_specs=[pl.BlockSpec((tl, S5_WIDTH), lambda b, t: (t, b)),
                   tok_spec(S5_WIDTH), tok_spec(D_MODEL), tok_spec(D_MODEL)],
        out_shape=[jax.ShapeDtypeStruct((seq, bsz * S5_WIDTH), F32),
                   jax.ShapeDtypeStruct((bsz * seq, S5_WIDTH), BF16),
                   jax.ShapeDtypeStruct((bsz * seq, D_MODEL), BF16),
                   jax.ShapeDtypeStruct((bsz * seq, D_MODEL), BF16)],
        scratch_shapes=[pltpu.VMEM((tl + SUBLANES, SSD_CONV_DIM), F32),
                        pltpu.VMEM((tl, SSD_CONV_DIM), F32),
                        pltpu.VMEM((tl, SSD_WIDTH), F32),
                        pltpu.VMEM((SSD_STATE, SSD_WIDTH), F32)],
        compiler_params=pltpu.CompilerParams(dimension_semantics=("arbitrary", "arbitrary"),
                                             vmem_limit_bytes=VMEM_LIMIT_BYTES),
        name="mixer_in",
    )(x, *consts)


def _s5_scan_kernel(u_ref, bh_ref, ch_ref, lam_ref, d_ref, wglu_ref, bglu_ref, y_ref,
                    st_ref, bu_ref, s_ref, yp_ref, *, bsz):
    steps = u_ref.shape[0] // bsz

    @pl.when(pl.program_id(0) == 0)
    def _():
        st_ref[...] = jnp.zeros(st_ref.shape, F32)

    u = u_ref[...]
    ub = u.astype(BF16)
    for q in range(S5_HALVES):
        bu_ref[...] = _dot(ub[:, q * S5_HCH:(q + 1) * S5_HCH], bh_ref[q])
        for k in range(S5_HST // S5_CW):
            re = slice(k * S5_CW, (k + 1) * S5_CW)
            im = slice(S5_HST + k * S5_CW, S5_HST + (k + 1) * S5_CW)
            lr = jnp.broadcast_to(lam_ref[q:q + 1, re], (bsz, S5_CW))
            li = jnp.broadcast_to(lam_ref[q:q + 1, im], (bsz, S5_CW))

            def body(t, carry, re=re, im=im, lr=lr, li=li):
                sr, si = carry
                r = pl.ds(pl.multiple_of(t * bsz, bsz), bsz)
                nr = lr * sr - li * si + bu_ref[r, re]
                ni = lr * si + li * sr + bu_ref[r, im]
                s_ref[r, re] = nr.astype(BF16)
                s_ref[r, im] = ni.astype(BF16)
                return nr, ni

            sr, si = lax.fori_loop(0, steps, body, (st_ref[q, :, re], st_ref[q, :, im]), unroll=4)
            st_ref[q, :, re] = sr
            st_ref[q, :, im] = si
        yp_ref[:, q * S5_HCH:(q + 1) * S5_HCH] = _dot(s_ref[...], ch_ref[q])
    y = jax.nn.gelu(yp_ref[...] + d_ref[...] * u)
    y = y * _sigmoid(_dot(y.astype(BF16), wglu_ref[...]) + bglu_ref[...])
    y_ref[...] = y.astype(BF16)


def _s5_scan(u_tm, bh, ch, lam, s5_d, w_glu, b_glu, bsz):
    rows_total = u_tm.shape[0]
    rows = TT_S5 * bsz
    consts = [bh, ch, lam, s5_d.reshape(1, -1), w_glu.astype(BF16), b_glu.reshape(1, -1)]
    return pl.pallas_call(
        functools.partial(_s5_scan_kernel, bsz=bsz),
        grid=(rows_total // rows,),
        in_specs=[pl.BlockSpec((rows, S5_WIDTH), lambda i: (i, 0))] + [_const_spec(c.shape) for c in consts],
        out_specs=pl.BlockSpec((rows, S5_WIDTH), lambda i: (i, 0)),
        out_shape=jax.ShapeDtypeStruct((rows_total, S5_WIDTH), BF16),
        scratch_shapes=[pltpu.VMEM((S5_HALVES, bsz, 2 * S5_HST), F32),
                        pltpu.VMEM((rows, 2 * S5_HST), F32),
                        pltpu.VMEM((rows, 2 * S5_HST), BF16),
                        pltpu.VMEM((rows, S5_WIDTH), F32)],
        compiler_params=pltpu.CompilerParams(dimension_semantics=("arbitrary",),
                                             vmem_limit_bytes=VMEM_LIMIT_BYTES),
        name="s5_scan",
    )(u_tm, *consts)


def _merge_out_kernel(x_ref, p_ref, y5_ref, zs5_ref, g5_ref, ms_ref, wb5_ref, wout_ref, pnw_ref,
                      wpg_ref, wpp_ref, fnw_ref, o_ref):
    y5 = (y5_ref[...].astype(F32) * zs5_ref[...].astype(F32)).astype(BF16)
    merged = g5_ref[...].astype(F32) * _dot(y5, wb5_ref[...]) + ms_ref[...].astype(F32)
    h = x_ref[...] + _dot(merged.astype(BF16), wout_ref[...])
    gate = _sigmoid(_dot(_rms(h, pnw_ref[...]).astype(BF16), wpg_ref[...]))
    h = h + gate * _dot(p_ref[...].astype(BF16), wpp_ref[...])
    o_ref[...] = _rms(h, fnw_ref[...])


def _merge_out(x, p, y5_tm, zs5, g5, ms, w_br_s5, w_out, ple_norm_w, w_ple_gate, w_ple_proj, final_norm_w):
    bsz, seq, d = x.shape
    tl = TL_OUT
    consts = [w_br_s5.astype(BF16), w_out.astype(BF16), ple_norm_w.reshape(1, d), w_ple_gate.astype(BF16),
              w_ple_proj.astype(BF16), final_norm_w.reshape(1, d)]
    tok_spec = lambda w: pl.BlockSpec((tl, w), lambda b, t: (b * (seq // tl) + t, 0))
    return pl.pallas_call(
        _merge_out_kernel,
        grid=(bsz, seq // tl),
        in_specs=[pl.BlockSpec((None, tl, d), lambda b, t: (b, t, 0)),
                  pl.BlockSpec((None, tl, PLE_DIM), lambda b, t: (b, t, 0)),
                  pl.BlockSpec((tl, S5_WIDTH), lambda b, t: (t, b)),
                  tok_spec(S5_WIDTH), tok_spec(D_MODEL), tok_spec(D_MODEL)]
                 + [_const_spec(c.shape) for c in consts],
        out_specs=pl.BlockSpec((None, tl, d), lambda b, t: (b, t, 0)),
        out_shape=jax.ShapeDtypeStruct((bsz, seq, d), F32),
        compiler_params=pltpu.CompilerParams(dimension_semantics=("parallel", "parallel"),
                                             vmem_limit_bytes=VMEM_LIMIT_BYTES),
        name="merge_out",
    )(x, p, y5_tm, zs5, g5, ms, *consts)


def kernel(x, p, norm_w, w_in, s5_a_re, s5_a_im, s5_b_re, s5_b_im, s5_c_re, s5_c_im, s5_d, s5_log_step, s5_w_glu, s5_b_glu, ssd_conv_w, ssd_conv_b, ssd_dt_bias, ssd_a_log, ssd_d, ssd_norm_w, w_br_s5, w_br_ssd, w_out, ple_norm_w, w_ple_gate, w_ple_proj, final_norm_w):
    bsz, seq, _ = x.shape
    i = 0
    assert norm_w.shape[0] == 1

    lre, lim, bbre, bbim, ncim = _s5_discretize(s5_a_re[i], s5_a_im[i], s5_log_step[i], s5_b_re[i],
                                                s5_b_im[i], s5_c_im[i])
    c_re_t = jnp.transpose(s5_c_re[i], (1, 0, 2)).reshape(S5_GROUP, S5_NSTATE)
    bh = jnp.concatenate([_s5_block_diag(bbre), _s5_block_diag(bbim)], axis=-1).astype(BF16)
    ch = jnp.concatenate([_s5_block_diag(c_re_t), _s5_block_diag(ncim)], axis=-1)
    ch = jnp.transpose(ch, (0, 2, 1)).astype(BF16)
    lam = jnp.concatenate([lre.reshape(S5_HALVES, S5_HST), lim.reshape(S5_HALVES, S5_HST)], axis=-1)

    u_tm, zs5, g5, ms = _mixer_in(x, norm_w[i], w_in[i], ssd_conv_w[i], ssd_conv_b[i], ssd_dt_bias[i],
                                  ssd_a_log[i], ssd_d[i], ssd_norm_w[i], w_br_ssd[i])
    y5_tm = _s5_scan(u_tm.reshape(seq * bsz, S5_WIDTH), bh, ch, lam, s5_d[i], s5_w_glu[i], s5_b_glu[i], bsz)
    return _merge_out(x, p[i], y5_tm.reshape(seq, bsz * S5_WIDTH), zs5, g5, ms, w_br_s5[i], w_out[i],
                      ple_norm_w[i], w_ple_gate[i], w_ple_proj[i], final_norm_w)
```

```python
import functools

import jax
import jax.numpy as jnp
from jax import lax
from jax.experimental import pallas as pl
from jax.experimental.pallas import tpu as pltpu

F32 = jnp.float32
BF16 = jnp.bfloat16

D_MODEL = 1024
PLE_DIM = 256
RMS_EPS = 1e-6
LOG2_E = 1.4426950408889634
S5_WIDTH = 512
S5_GROUP = 16
S5_GROUPS = S5_WIDTH // S5_GROUP
S5_STATE = 64
S5_NSTATE = S5_GROUPS * S5_STATE
S5_HALVES = 2
S5_HCH = S5_WIDTH // S5_HALVES
S5_HST = S5_NSTATE // S5_HALVES
SSD_WIDTH = 1536
SSD_HEADDIM = 64
SSD_HEADS = SSD_WIDTH // SSD_HEADDIM
SSD_GROUPS = 4
SSD_HPG = SSD_HEADS // SSD_GROUPS
SSD_STATE = 128
SSD_CONV = 4
SSD_BC = SSD_GROUPS * SSD_STATE
SSD_CONV_DIM = SSD_WIDTH + 2 * SSD_BC
SSD_GW = SSD_WIDTH // SSD_GROUPS
N_BRANCH = 2

MXU_COLS = 256
LANES = 128
SUBLANES = 8
VMEM_LIMIT_BYTES = 56 * 1024 * 1024

Q = 128
TL_IN = 256
TT_S5 = 32
S5_CW = 256
TL_OUT = 512


def _sigmoid(v):
    return jax.nn.sigmoid(v)


def _silu(v):
    return v * jax.nn.sigmoid(v)


def _softplus(v):
    return jnp.maximum(v, 0.0) + jnp.log1p(jnp.exp(-jnp.abs(v)))


def _rms(v, w):
    return v * lax.rsqrt(jnp.mean(v * v, axis=-1, keepdims=True) + RMS_EPS) * w


def _dot(a, b):
    return jnp.dot(a, b, preferred_element_type=F32)


def _s5_disc_kernel(are_ref, aim_ref, lstep_ref, bre_ref, bim_ref, cim_ref,
                    lre_ref, lim_ref, bbre_ref, bbim_ref, ncim_ref):
    a_re = are_ref[...]
    a_im = aim_ref[...]
    step = jnp.exp(lstep_ref[...])
    mag = jnp.exp(a_re * step)
    lb_re = mag * jnp.cos(a_im * step)
    lb_im = mag * jnp.sin(a_im * step)
    den = a_re * a_re + a_im * a_im
    n_re = lb_re - 1.0
    n_im = lb_im
    f_re = (n_re * a_re + n_im * a_im) / den
    f_im = (n_im * a_re - n_re * a_im) / den
    b_re = bre_ref[...]
    b_im = bim_ref[...]
    lre_ref[...] = lb_re
    lim_ref[...] = lb_im
    bbre_ref[...] = f_re * b_re - f_im * b_im
    bbim_ref[...] = f_re * b_im + f_im * b_re
    ncim_ref[...] = -cim_ref[...]


def _s5_discretize(a_re, a_im, log_step, b_re, b_im, c_im):
    n = S5_NSTATE
    row = lambda v: v.reshape(1, n)
    b_t = lambda v: jnp.transpose(v, (2, 0, 1)).reshape(S5_GROUP, n)
    c_t = lambda v: jnp.transpose(v, (1, 0, 2)).reshape(S5_GROUP, n)
    lstep = jnp.repeat(log_step, S5_STATE).reshape(1, n)
    outs = pl.pallas_call(
        _s5_disc_kernel,
        out_shape=[jax.ShapeDtypeStruct((1, n), F32), jax.ShapeDtypeStruct((1, n), F32),
                   jax.ShapeDtypeStruct((S5_GROUP, n), F32), jax.ShapeDtypeStruct((S5_GROUP, n), F32),
                   jax.ShapeDtypeStruct((S5_GROUP, n), F32)],
        name="s5_discretize",
    )(row(a_re), row(a_im), lstep, b_t(b_re), b_t(b_im), c_t(c_im))
    return outs


def _s5_block_diag(m_t):
    gl = S5_GROUPS // S5_HALVES
    m4 = m_t.reshape(S5_GROUP, S5_HALVES, gl, S5_STATE)
    eye = jnp.eye(gl, dtype=m_t.dtype)
    out = jnp.einsum('hqgp,gk->qghkp', m4, eye)
    return out.reshape(S5_HALVES, S5_HCH, S5_HST)


def _run_interleaved(streams):
    acc = [0.0] * len(streams)
    live = list(range(len(streams)))
    while live:
        i = min(live, key=lambda j: acc[j])
        try:
            acc[i] += next(streams[i])
        except StopIteration:
            live.remove(i)


def _proj_items(x_ref, nw_ref, wu_ref, wz5_ref, wzs_ref, wxbc_ref, wdt_ref, wg_ref, dtb_ref,
                u_ref, zs5_ref, g5_ref, hb_ref, rows, slot):
    xraw_ref, dt_ref, zs_ref, gss_ref = slot
    tl = hb_ref.shape[0]
    hb_ref[...] = _rms(x_ref[rows, :], nw_ref[...]).astype(BF16)
    yield 300

    def blocks(w_ref):
        n = w_ref.shape[1]
        for c0 in range(0, n, MXU_COLS):
            cs = slice(c0, min(c0 + MXU_COLS, n))
            yield cs, _dot(hb_ref[...], w_ref[:, cs])

    cost = tl
    for cs, r in blocks(wxbc_ref):
        xraw_ref[SUBLANES:SUBLANES + tl, cs] = r
        yield cost
    for cs, r in blocks(wdt_ref):
        dt_ref[:, cs] = _softplus(r + dtb_ref[:, cs])
        yield cost
    for cs, r in blocks(wu_ref):
        u_ref[rows, cs] = r
        yield cost
    for cs, r in blocks(wz5_ref):
        zs5_ref[rows, cs] = _silu(r).astype(BF16)
        yield cost
    for cs, r in blocks(wg_ref):
        g = _sigmoid(r)
        if cs.start < D_MODEL:
            g5_ref[rows, cs] = g.astype(BF16)
        else:
            gss_ref[:, cs.start - D_MODEL:cs.stop - D_MODEL] = g
        yield cost
    for cs, r in blocks(wzs_ref):
        zs_ref[:, cs] = r
        yield cost


def _cumsum_rows(tril_b, v):
    hi = v.astype(BF16)
    r1 = v - hi.astype(F32)
    mid = r1.astype(BF16)
    lo = (r1 - mid.astype(F32)).astype(BF16)
    w = v.shape[1]
    s = _dot(tril_b, jnp.concatenate([hi, mid, lo], axis=1))
    return s[:, :w] + s[:, w:2 * w] + s[:, 2 * w:]


def _ssd_items(slot, next_slot, convw_ref, convb_ref, alog_ref, dskip_ref, snw_ref, wbs_ref,
               ms_ref, xc_ref, y_ref, yn_ref, state_ref):
    xraw_ref, dt_ref, zs_ref, gss_ref = slot
    tl = dt_ref.shape[0]
    tail = SUBLANES

    cblk = 256
    for c0 in range(0, SSD_CONV_DIM, cblk):
        cs = slice(c0, c0 + cblk)
        acc = convb_ref[:, cs]
        for k in range(SSD_CONV):
            off = tail - (SSD_CONV - 1) + k
            acc = acc + convw_ref[k:k + 1, cs] * xraw_ref[off:off + tl, cs]
        xc_ref[:, cs] = _silu(acc)
        yield 320
    next_slot[0][0:tail, :] = xraw_ref[tl:tl + tail, :]

    a_row = -jnp.exp(alog_ref[...])
    ri = lax.broadcasted_iota(jnp.int32, (Q, Q), 0)
    ci = lax.broadcasted_iota(jnp.int32, (Q, Q), 1)
    causal = ci <= ri
    tril_b = causal.astype(BF16)
    lane_lo = lax.broadcasted_iota(jnp.int32, (Q, LANES), 1) < SSD_HEADDIM
    b_off = SSD_WIDTH
    c_off = SSD_WIDTH + SSD_BC

    for c in range(tl // Q):
        rows = slice(c * Q, (c + 1) * Q)
        dt = dt_ref[rows, :]
        a_cum = _cumsum_rows(tril_b, dt * a_row) * LOG2_E
        a_cum_t = a_cum.T
        dt_t = dt.T
        ea_all = jnp.exp2(a_cum)
        wcol_all = jnp.exp2(a_cum[Q - 1:Q, :] - a_cum) * dt
        yield 100
        for g in range(SSD_GROUPS):
            bg = xc_ref[rows, b_off + g * SSD_STATE:b_off + (g + 1) * SSD_STATE].astype(BF16)
            cg = xc_ref[rows, c_off + g * SSD_STATE:c_off + (g + 1) * SSD_STATE].astype(BF16)
            scores = lax.dot_general(cg, bg, (((1,), (1,)), ((), ())), preferred_element_type=F32)
            gcols = slice(g * SSD_GW, (g + 1) * SSD_GW)
            y_off = _dot(cg, state_ref[:, gcols].astype(BF16))
            yield 100
            xw_parts = []
            decay_parts = []
            for k in range(SSD_HPG // 2):
                cols = slice(g * SSD_GW + k * LANES, g * SSD_GW + (k + 1) * LANES)
                xs = xc_ref[rows, cols]
                wts, ea, wcol = [], [], []
                for h in (g * SSD_HPG + 2 * k, g * SSD_HPG + 2 * k + 1):
                    seg = jnp.broadcast_to(a_cum[:, h:h + 1], (Q, Q)) - a_cum_t[h:h + 1, :]
                    lmat = jnp.exp2(jnp.where(causal, seg, -jnp.inf))
                    wts.append((scores * lmat * dt_t[h:h + 1, :]).astype(BF16))
                    ea.append(jnp.broadcast_to(ea_all[:, h:h + 1], (Q, LANES)))
                    wcol.append(jnp.broadcast_to(wcol_all[:, h:h + 1], (Q, LANES)))
                xb = xs.astype(BF16)
                zero = jnp.zeros_like(xb)
                rhs = jnp.concatenate([jnp.where(lane_lo, xb, zero), jnp.where(lane_lo, zero, xb)], axis=0)
                y_diag = _dot(jnp.concatenate(wts, axis=1), rhs)
                ea2 = jnp.where(lane_lo, ea[0], ea[1])
                wcol2 = jnp.where(lane_lo, wcol[0], wcol[1])
                y_ref[rows, cols] = (y_diag + y_off[:, k * LANES:(k + 1) * LANES] * ea2
                                     + xs * dskip_ref[:, cols])
                xw_parts.append((xs * wcol2).astype(BF16))
                decay_parts.append(ea2[Q - 1:Q, :])
                yield 220
            xw = jnp.concatenate(xw_parts, axis=1)
            new = lax.dot_general(bg, xw, (((0,), (0,)), ((), ())), preferred_element_type=F32)
            decay = jnp.concatenate(decay_parts, axis=1)
            state_ref[:, gcols] = state_ref[:, gcols] * decay + new
            yield 100

    for g in range(SSD_GROUPS):
        gcols = slice(g * SSD_GW, (g + 1) * SSD_GW)
        yz = y_ref[:, gcols] * _silu(zs_ref[:, gcols])
        yn_ref[:, gcols] = _rms(yz, snw_ref[:, gcols]).astype(BF16)
        yield 300
    for c0 in range(0, D_MODEL, MXU_COLS):
        cs = slice(c0, c0 + MXU_COLS)
        ms_ref[:, cs] = (gss_ref[:, cs] * _dot(yn_ref[...], wbs_ref[:, cs])).astype(BF16)
        yield 400


def _mixer_in_kernel(x_ref, nw_ref, wu_ref, wz5_ref, wzs_ref, wxbc_ref, wdt_ref, wg_ref,
                     convw_ref, convb_ref, dtb_ref, alog_ref, dskip_ref, snw_ref, wbs_ref,
                     u_ref, zs5_ref, g5_ref, mso_ref, mse_ref,
                     xraw0, dt0, zs0, gss0, xraw1, dt1, zs1, gss1, hb_ref, xc_ref, y_ref, yn_ref, state_ref,
                     *, tiles_per_seq):
    m = pl.program_id(0)
    tl = dt0.shape[0]
    slots = ((xraw0, dt0, zs0, gss0), (xraw1, dt1, zs1, gss1))

    @pl.when(m == 0)
    def _():
        for ref in slots[0]:
            ref[...] = jnp.zeros(ref.shape, ref.dtype)
        state_ref[...] = jnp.zeros(state_ref.shape, F32)

    proj = functools.partial(_proj_items, x_ref, nw_ref, wu_ref, wz5_ref, wzs_ref, wxbc_ref, wdt_ref, wg_ref,
                             dtb_ref, u_ref, zs5_ref, g5_ref, hb_ref)
    ssd = functools.partial(_ssd_items, convw_ref=convw_ref, convb_ref=convb_ref, alog_ref=alog_ref,
                            dskip_ref=dskip_ref, snw_ref=snw_ref, wbs_ref=wbs_ref,
                            xc_ref=xc_ref, y_ref=y_ref, yn_ref=yn_ref, state_ref=state_ref)

    _run_interleaved([ssd(slots[0], slots[1], ms_ref=mso_ref), proj(rows=slice(0, tl), slot=slots[1])])

    @pl.when(lax.rem(2 * m, tiles_per_seq) == 0)
    def _():
        xraw1[0:SUBLANES, :] = jnp.zeros((SUBLANES, SSD_CONV_DIM), F32)
        state_ref[...] = jnp.zeros(state_ref.shape, F32)

    _run_interleaved([ssd(slots[1], slots[0], ms_ref=mse_ref), proj(rows=slice(tl, 2 * tl), slot=slots[0])])


def _const_spec(shape):
    zeros = (0,) * len(shape)
    return pl.BlockSpec(shape, lambda *_: zeros)


def _mixer_in(x, norm_w, w_in, conv_w, conv_b, dt_bias, a_log, d_skip, ssd_norm_w, w_br_ssd):
    bsz, seq, d = x.shape
    tl = TL_IN
    assert seq % (2 * tl) == 0 and 2 * tl == TL_OUT
    rows_total = bsz * seq
    nsteps = rows_total // (2 * tl)
    sizes = (S5_WIDTH, S5_WIDTH, SSD_WIDTH, SSD_CONV_DIM, SSD_HEADS, N_BRANCH * D_MODEL)
    offs = [0]
    for s in sizes:
        offs.append(offs[-1] + s)
    wb = w_in.astype(BF16)
    wu, wz5, wzs, wxbc, wdt, wg = (wb[:, offs[i]:offs[i + 1]] for i in range(6))
    pad = LANES - SSD_HEADS
    wdt = jnp.pad(wdt, ((0, 0), (0, pad)))
    dtb = jnp.pad(dt_bias.reshape(1, SSD_HEADS), ((0, 0), (0, pad)))
    alog = jnp.pad(a_log.reshape(1, SSD_HEADS), ((0, 0), (0, pad)))
    dskip = jnp.repeat(d_skip, SSD_HEADDIM).reshape(1, SSD_WIDTH)
    consts = [norm_w.reshape(1, d), wu, wz5, wzs, wxbc, wdt, wg, conv_w, conv_b.reshape(1, -1), dtb, alog,
              dskip, ssd_norm_w.reshape(1, -1), w_br_ssd.astype(BF16)]
    proj_spec = lambda w: pl.BlockSpec((2 * tl, w), lambda m: (jnp.minimum(m, nsteps - 1), 0))
    slot = [pltpu.VMEM((tl + SUBLANES, SSD_CONV_DIM), F32), pltpu.VMEM((tl, LANES), F32),
            pltpu.VMEM((tl, SSD_WIDTH), F32), pltpu.VMEM((tl, D_MODEL), F32)]
    return pl.pallas_call(
        functools.partial(_mixer_in_kernel, tiles_per_seq=seq // tl),
        grid=(nsteps + 1,),
        in_specs=[proj_spec(d)] + [_const_spec(c.shape) for c in consts],
        out_specs=[proj_spec(S5_WIDTH), proj_spec(S5_WIDTH), proj_spec(D_MODEL),
                   pl.BlockSpec((tl, D_MODEL), lambda m: (jnp.maximum(m - 1, 0), 0)),
                   pl.BlockSpec((tl, D_MODEL), lambda m: (m, 0))],
        out_shape=[jax.ShapeDtypeStruct((rows_total, S5_WIDTH), F32),
                   jax.ShapeDtypeStruct((rows_total, S5_WIDTH), BF16),
                   jax.ShapeDtypeStruct((rows_total, D_MODEL), BF16),
                   jax.ShapeDtypeStruct((nsteps * tl, D_MODEL), BF16),
                   jax.ShapeDtypeStruct(((nsteps + 1) * tl, D_MODEL), BF16)],
        scratch_shapes=slot + slot + [pltpu.VMEM((tl, D_MODEL), BF16),
                                      pltpu.VMEM((tl, SSD_CONV_DIM), F32),
                                      pltpu.VMEM((tl, SSD_WIDTH), F32),
                                      pltpu.VMEM((tl, SSD_WIDTH), BF16),
                                      pltpu.VMEM((SSD_STATE, SSD_WIDTH), F32)],
        compiler_params=pltpu.CompilerParams(dimension_semantics=("arbitrary",),
                                             vmem_limit_bytes=VMEM_LIMIT_BYTES),
        name="mixer_in",
    )(x.reshape(rows_total, d), *consts)


def _s5_scan_kernel(u_ref, bh_ref, ch_ref, lam_ref, d_ref, wglu_ref, bglu_ref, y_ref,
                    st_ref, utm_ref, bu_ref, s_ref, ytm_ref):
    bsz, steps, _ = u_ref.shape

    @pl.when(pl.program_id(0) == 0)
    def _():
        st_ref[...] = jnp.zeros(st_ref.shape, F32)

    nlt = utm_ref.shape[0]
    for b in range(bsz):
        for c in range(nlt):
            utm_ref[c, pl.ds(b, steps, stride=bsz), :] = u_ref[b, :, c * LANES:(c + 1) * LANES]
    u = jnp.concatenate([utm_ref[c] for c in range(nlt)], axis=1)
    ub = u.astype(BF16)
    for q in range(S5_HALVES):
        bu_ref[q] = _dot(ub[:, q * S5_HCH:(q + 1) * S5_HCH], bh_ref[q])
    ys = []
    for q in range(S5_HALVES):
        for k in range(S5_HST // S5_CW):
            re = slice(k * S5_CW, (k + 1) * S5_CW)
            im = slice(S5_HST + k * S5_CW, S5_HST + (k + 1) * S5_CW)
            lr = jnp.broadcast_to(lam_ref[q:q + 1, re], (bsz, S5_CW))
            li = jnp.broadcast_to(lam_ref[q:q + 1, im], (bsz, S5_CW))
            sr = st_ref[q, :, re]
            si = st_ref[q, :, im]
            for t in range(steps):
                r = slice(t * bsz, (t + 1) * bsz)
                sr, si = (lr * sr - li * si + bu_ref[q, r, re],
                          lr * si + li * sr + bu_ref[q, r, im])
                s_ref[q, r, re] = sr.astype(BF16)
                s_ref[q, r, im] = si.astype(BF16)
            st_ref[q, :, re] = sr
            st_ref[q, :, im] = si
        ys.append(_dot(s_ref[q], ch_ref[q]))
    y = jax.nn.gelu(jnp.concatenate(ys, axis=1) + d_ref[...] * u)
    y = y * _sigmoid(_dot(y.astype(BF16), wglu_ref[...]) + bglu_ref[...])
    for c in range(nlt):
        ytm_ref[c] = y[:, c * LANES:(c + 1) * LANES]
    for b in range(bsz):
        for c in range(nlt):
            y_ref[b, :, c * LANES:(c + 1) * LANES] = ytm_ref[c, pl.ds(b, steps, stride=bsz), :].astype(BF16)


def _s5_scan(u, bh, ch, lam, s5_d, w_glu, b_glu):
    bsz, seq, _ = u.shape
    steps = TT_S5
    rows = steps * bsz
    consts = [bh, ch, lam, s5_d.reshape(1, -1), w_glu.astype(BF16), b_glu.reshape(1, -1)]
    return pl.pallas_call(
        _s5_scan_kernel,
        grid=(seq // steps,),
        in_specs=[pl.BlockSpec((bsz, steps, S5_WIDTH), lambda i: (0, i, 0))]
                 + [_const_spec(c.shape) for c in consts],
        out_specs=pl.BlockSpec((bsz, steps, S5_WIDTH), lambda i: (0, i, 0)),
        out_shape=jax.ShapeDtypeStruct((bsz, seq, S5_WIDTH), BF16),
        scratch_shapes=[pltpu.VMEM((S5_HALVES, bsz, 2 * S5_HST), F32),
                        pltpu.VMEM((S5_WIDTH // LANES, rows, LANES), F32),
                        pltpu.VMEM((S5_HALVES, rows, 2 * S5_HST), F32),
                        pltpu.VMEM((S5_HALVES, rows, 2 * S5_HST), BF16),
                        pltpu.VMEM((S5_WIDTH // LANES, rows, LANES), F32)],
        compiler_params=pltpu.CompilerParams(dimension_semantics=("arbitrary",),
                                             vmem_limit_bytes=VMEM_LIMIT_BYTES),
        name="s5_scan",
    )(u, *consts)


def _merge_out_kernel(x_ref, p_ref, y5_ref, zs5_ref, g5_ref, mse_ref, mso_ref, wb5_ref, wout_ref, pnw_ref,
                      wpg_ref, wpp_ref, fnw_ref, o_ref):
    y5 = (y5_ref[...].astype(F32) * zs5_ref[...].astype(F32)).astype(BF16)
    ms = jnp.concatenate([mse_ref[...], mso_ref[...]], axis=0)
    merged = g5_ref[...].astype(F32) * _dot(y5, wb5_ref[...]) + ms.astype(F32)
    h = x_ref[...] + _dot(merged.astype(BF16), wout_ref[...])
    gate = _sigmoid(_dot(_rms(h, pnw_ref[...]).astype(BF16), wpg_ref[...]))
    h = h + gate * _dot(p_ref[...].astype(BF16), wpp_ref[...])
    o_ref[...] = _rms(h, fnw_ref[...])


def _merge_out(x, p, y5, zs5, g5, ms_even, ms_odd, w_br_s5, w_out, ple_norm_w, w_ple_gate, w_ple_proj,
               final_norm_w):
    bsz, seq, d = x.shape
    tl = TL_OUT
    consts = [w_br_s5.astype(BF16), w_out.astype(BF16), ple_norm_w.reshape(1, d), w_ple_gate.astype(BF16),
              w_ple_proj.astype(BF16), final_norm_w.reshape(1, d)]
    tok_spec = lambda w, rows=tl: pl.BlockSpec((rows, w), lambda b, t: (b * (seq // tl) + t, 0))
    return pl.pallas_call(
        _merge_out_kernel,
        grid=(bsz, seq // tl),
        in_specs=[pl.BlockSpec((None, tl, d), lambda b, t: (b, t, 0)),
                  pl.BlockSpec((None, tl, PLE_DIM), lambda b, t: (b, t, 0)),
                  tok_spec(S5_WIDTH), tok_spec(S5_WIDTH), tok_spec(D_MODEL),
                  tok_spec(D_MODEL, tl // 2), tok_spec(D_MODEL, tl // 2)]
                 + [_const_spec(c.shape) for c in consts],
        out_specs=pl.BlockSpec((None, tl, d), lambda b, t: (b, t, 0)),
        out_shape=jax.ShapeDtypeStruct((bsz, seq, d), F32),
        compiler_params=pltpu.CompilerParams(dimension_semantics=("parallel", "parallel"),
                                             vmem_limit_bytes=VMEM_LIMIT_BYTES),
        name="merge_out",
    )(x, p, y5, zs5, g5, ms_even, ms_odd, *consts)


def kernel(x, p, norm_w, w_in, s5_a_re, s5_a_im, s5_b_re, s5_b_im, s5_c_re, s5_c_im, s5_d, s5_log_step, s5_w_glu, s5_b_glu, ssd_conv_w, ssd_conv_b, ssd_dt_bias, ssd_a_log, ssd_d, ssd_norm_w, w_br_s5, w_br_ssd, w_out, ple_norm_w, w_ple_gate, w_ple_proj, final_norm_w):
    bsz, seq, _ = x.shape
    i = 0
    assert norm_w.shape[0] == 1

    lre, lim, bbre, bbim, ncim = _s5_discretize(s5_a_re[i], s5_a_im[i], s5_log_step[i], s5_b_re[i],
                                                s5_b_im[i], s5_c_im[i])
    c_re_t = jnp.transpose(s5_c_re[i], (1, 0, 2)).reshape(S5_GROUP, S5_NSTATE)
    bh = jnp.concatenate([_s5_block_diag(bbre), _s5_block_diag(bbim)], axis=-1).astype(BF16)
    ch = jnp.concatenate([_s5_block_diag(c_re_t), _s5_block_diag(ncim)], axis=-1)
    ch = jnp.transpose(ch, (0, 2, 1)).astype(BF16)
    lam = jnp.concatenate([lre.reshape(S5_HALVES, S5_HST), lim.reshape(S5_HALVES, S5_HST)], axis=-1)

    u, zs5, g5, ms_odd, ms_even = _mixer_in(x, norm_w[i], w_in[i], ssd_conv_w[i], ssd_conv_b[i],
                                            ssd_dt_bias[i], ssd_a_log[i], ssd_d[i], ssd_norm_w[i], w_br_ssd[i])
    y5 = _s5_scan(u.reshape(bsz, seq, S5_WIDTH), bh, ch, lam, s5_d[i], s5_w_glu[i], s5_b_glu[i])
    return _merge_out(x, p[i], y5.reshape(bsz * seq, S5_WIDTH), zs5, g5, ms_even, ms_odd, w_br_s5[i],
                      w_out[i], ple_norm_w[i], w_ple_gate[i], w_ple_proj[i], final_norm_w)
```

```python
import functools

import jax
import jax.numpy as jnp
from jax import lax
from jax.experimental import pallas as pl
from jax.experimental.pallas import tpu as pltpu

F32 = jnp.float32
BF16 = jnp.bfloat16

D_MODEL = 1024
PLE_DIM = 256
RMS_EPS = 1e-6
LOG2_E = 1.4426950408889634
S5_WIDTH = 512
S5_GROUP = 16
S5_GROUPS = S5_WIDTH // S5_GROUP
S5_STATE = 64
S5_NSTATE = S5_GROUPS * S5_STATE
S5_HALVES = 2
S5_HCH = S5_WIDTH // S5_HALVES
S5_HST = S5_NSTATE // S5_HALVES
SSD_WIDTH = 1536
SSD_HEADDIM = 64
SSD_HEADS = SSD_WIDTH // SSD_HEADDIM
SSD_GROUPS = 4
SSD_HPG = SSD_HEADS // SSD_GROUPS
SSD_STATE = 128
SSD_CONV = 4
SSD_BC = SSD_GROUPS * SSD_STATE
SSD_CONV_DIM = SSD_WIDTH + 2 * SSD_BC
SSD_GW = SSD_WIDTH // SSD_GROUPS
N_BRANCH = 2

MXU_COLS = 256
LANES = 128
SUBLANES = 8
VMEM_LIMIT_BYTES = 56 * 1024 * 1024

Q = 128
TL_IN = 256
TT_S5 = 32
S5_CW = 256
TL_OUT = 512


def _sigmoid(v):
    return jax.nn.sigmoid(v)


def _silu(v):
    return v * jax.nn.sigmoid(v)


def _softplus(v):
    return jnp.maximum(v, 0.0) + jnp.log1p(jnp.exp(-jnp.abs(v)))


def _rms(v, w):
    return v * lax.rsqrt(jnp.mean(v * v, axis=-1, keepdims=True) + RMS_EPS) * w


def _dot(a, b):
    return jnp.dot(a, b, preferred_element_type=F32)


def _s5_disc_kernel(are_ref, aim_ref, lstep_ref, bre_ref, bim_ref, cim_ref,
                    lre_ref, lim_ref, bbre_ref, bbim_ref, ncim_ref):
    a_re = are_ref[...]
    a_im = aim_ref[...]
    step = jnp.exp(lstep_ref[...])
    mag = jnp.exp(a_re * step)
    lb_re = mag * jnp.cos(a_im * step)
    lb_im = mag * jnp.sin(a_im * step)
    den = a_re * a_re + a_im * a_im
    n_re = lb_re - 1.0
    n_im = lb_im
    f_re = (n_re * a_re + n_im * a_im) / den
    f_im = (n_im * a_re - n_re * a_im) / den
    b_re = bre_ref[...]
    b_im = bim_ref[...]
    lre_ref[...] = lb_re
    lim_ref[...] = lb_im
    bbre_ref[...] = f_re * b_re - f_im * b_im
    bbim_ref[...] = f_re * b_im + f_im * b_re
    ncim_ref[...] = -cim_ref[...]


def _s5_discretize(a_re, a_im, log_step, b_re, b_im, c_im):
    n = S5_NSTATE
    row = lambda v: v.reshape(1, n)
    b_t = lambda v: jnp.transpose(v, (2, 0, 1)).reshape(S5_GROUP, n)
    c_t = lambda v: jnp.transpose(v, (1, 0, 2)).reshape(S5_GROUP, n)
    lstep = jnp.repeat(log_step, S5_STATE).reshape(1, n)
    outs = pl.pallas_call(
        _s5_disc_kernel,
        out_shape=[jax.ShapeDtypeStruct((1, n), F32), jax.ShapeDtypeStruct((1, n), F32),
                   jax.ShapeDtypeStruct((S5_GROUP, n), F32), jax.ShapeDtypeStruct((S5_GROUP, n), F32),
                   jax.ShapeDtypeStruct((S5_GROUP, n), F32)],
        name="s5_discretize",
    )(row(a_re), row(a_im), lstep, b_t(b_re), b_t(b_im), c_t(c_im))
    return outs


def _s5_block_diag(m_t):
    gl = S5_GROUPS // S5_HALVES
    m4 = m_t.reshape(S5_GROUP, S5_HALVES, gl, S5_STATE)
    eye = jnp.eye(gl, dtype=m_t.dtype)
    out = jnp.einsum('hqgp,gk->qghkp', m4, eye)
    return out.reshape(S5_HALVES, S5_HCH, S5_HST)


def _s5_pair_blocks(re, im):
    h, r, _ = re.shape
    nb = S5_HST // S5_CW
    both = jnp.stack([re.reshape(h, r, nb, S5_CW), im.reshape(h, r, nb, S5_CW)], axis=3)
    return both.reshape(h, r, 2 * S5_HST)


def _run_interleaved(streams):
    acc = [0.0] * len(streams)
    live = list(range(len(streams)))
    while live:
        i = min(live, key=lambda j: acc[j])
        try:
            acc[i] += next(streams[i])
        except StopIteration:
            live.remove(i)


def _proj_items(x_ref, nw_ref, wu_ref, wz5_ref, wzs_ref, wxbc_ref, wdt_ref, wg_ref, dtb_ref,
                u_ref, zs5_ref, g5_ref, hb_ref, rows, slot):
    xraw_ref, dt_ref, zs_ref, gss_ref = slot
    tl = hb_ref.shape[0]
    hb_ref[...] = _rms(x_ref[rows, :], nw_ref[...]).astype(BF16)
    yield 300

    def blocks(w_ref):
        n = w_ref.shape[1]
        for c0 in range(0, n, MXU_COLS):
            cs = slice(c0, min(c0 + MXU_COLS, n))
            yield cs, _dot(hb_ref[...], w_ref[:, cs])

    cost = tl
    for cs, r in blocks(wxbc_ref):
        for j in range(0, cs.stop - cs.start, LANES):
            xraw_ref[(cs.start + j) // LANES, SUBLANES:SUBLANES + tl, :] = r[:, j:j + LANES]
        yield cost
    for cs, r in blocks(wdt_ref):
        dt_ref[:, cs] = _softplus(r + dtb_ref[:, cs])
        yield cost
    for cs, r in blocks(wu_ref):
        u_ref[rows, cs] = r
        yield cost
    for cs, r in blocks(wz5_ref):
        zs5_ref[rows, cs] = _silu(r).astype(BF16)
        yield cost
    for cs, r in blocks(wg_ref):
        g = _sigmoid(r)
        if cs.start < D_MODEL:
            g5_ref[rows, cs] = g.astype(BF16)
        else:
            gss_ref[:, cs.start - D_MODEL:cs.stop - D_MODEL] = g
        yield cost
    for cs, r in blocks(wzs_ref):
        zs_ref[:, cs] = r
        yield cost


def _cumsum_rows(tril_b, v):
    hi = v.astype(BF16)
    r1 = v - hi.astype(F32)
    mid = r1.astype(BF16)
    lo = (r1 - mid.astype(F32)).astype(BF16)
    w = v.shape[1]
    s = _dot(tril_b, jnp.concatenate([hi, mid, lo], axis=1))
    return s[:, :w] + s[:, w:2 * w] + s[:, 2 * w:]


def _ssd_items(slot, next_slot, convw_ref, convb_ref, alog_ref, dskip_ref, snw_ref, wbs_ref,
               ms_ref, yn_ref, state_ref):
    xraw_ref, dt_ref, zs_ref, gss_ref = slot
    tl = dt_ref.shape[0]
    tail = SUBLANES

    def conv_tile(ct, r0):
        cs = slice(ct * LANES, (ct + 1) * LANES)
        acc = convb_ref[:, cs]
        for k in range(SSD_CONV):
            off = r0 + tail - (SSD_CONV - 1) + k
            acc = acc + convw_ref[k:k + 1, cs] * xraw_ref[ct, off:off + Q, :]
        return _silu(acc)

    a_row = -jnp.exp(alog_ref[...])
    ri = lax.broadcasted_iota(jnp.int32, (Q, Q), 0)
    ci = lax.broadcasted_iota(jnp.int32, (Q, Q), 1)
    causal = ci <= ri
    tril_b = causal.astype(BF16)
    lane_lo = lax.broadcasted_iota(jnp.int32, (Q, LANES), 1) < SSD_HEADDIM
    b_tile = SSD_WIDTH // LANES
    c_tile = (SSD_WIDTH + SSD_BC) // LANES

    for c in range(tl // Q):
        r0 = c * Q
        rows = slice(r0, r0 + Q)
        dt = dt_ref[rows, :]
        a_cum = _cumsum_rows(tril_b, dt * a_row) * LOG2_E
        a_cum_t = a_cum.T
        dt_t = dt.T
        ea_all = jnp.exp2(a_cum)
        wcol_all = jnp.exp2(a_cum[Q - 1:Q, :] - a_cum) * dt
        yield 100
        for g in range(SSD_GROUPS):
            bg = conv_tile(b_tile + g, r0).astype(BF16)
            cg = conv_tile(c_tile + g, r0).astype(BF16)
            scores = lax.dot_general(cg, bg, (((1,), (1,)), ((), ())), preferred_element_type=F32)
            gcols = slice(g * SSD_GW, (g + 1) * SSD_GW)
            y_off = _dot(cg, state_ref[:, gcols].astype(BF16))
            yield 150
            xw_parts = []
            decay_parts = []
            y_parts = []
            for k in range(SSD_HPG // 2):
                cols = slice(g * SSD_GW + k * LANES, g * SSD_GW + (k + 1) * LANES)
                xs = conv_tile(cols.start // LANES, r0)
                wts, ea, wcol = [], [], []
                for h in (g * SSD_HPG + 2 * k, g * SSD_HPG + 2 * k + 1):
                    seg = jnp.broadcast_to(a_cum[:, h:h + 1], (Q, Q)) - a_cum_t[h:h + 1, :]
                    lmat = jnp.exp2(jnp.where(causal, seg, -jnp.inf))
                    wts.append((scores * lmat * dt_t[h:h + 1, :]).astype(BF16))
                    ea.append(jnp.broadcast_to(ea_all[:, h:h + 1], (Q, LANES)))
                    wcol.append(jnp.broadcast_to(wcol_all[:, h:h + 1], (Q, LANES)))
                xb = xs.astype(BF16)
                zero = jnp.zeros_like(xb)
                rhs = jnp.concatenate([jnp.where(lane_lo, xb, zero), jnp.where(lane_lo, zero, xb)], axis=0)
                y_diag = _dot(jnp.concatenate(wts, axis=1), rhs)
                ea2 = jnp.where(lane_lo, ea[0], ea[1])
                wcol2 = jnp.where(lane_lo, wcol[0], wcol[1])
                y_parts.append(y_diag + y_off[:, k * LANES:(k + 1) * LANES] * ea2 + xs * dskip_ref[:, cols])
                xw_parts.append((xs * wcol2).astype(BF16))
                decay_parts.append(ea2[Q - 1:Q, :])
                yield 250
            xw = jnp.concatenate(xw_parts, axis=1)
            new = lax.dot_general(bg, xw, (((0,), (0,)), ((), ())), preferred_element_type=F32)
            decay = jnp.concatenate(decay_parts, axis=1)
            state_ref[:, gcols] = state_ref[:, gcols] * decay + new
            yz = jnp.concatenate(y_parts, axis=1) * _silu(zs_ref[rows, gcols])
            yn_ref[rows, gcols] = _rms(yz, snw_ref[:, gcols]).astype(BF16)
            yield 250
    next_slot[0][:, 0:tail, :] = xraw_ref[:, tl:tl + tail, :]

    for c0 in range(0, D_MODEL, MXU_COLS):
        cs = slice(c0, c0 + MXU_COLS)
        ms_ref[:, cs] = (gss_ref[:, cs] * _dot(yn_ref[...], wbs_ref[:, cs])).astype(BF16)
        yield 400


def _mixer_in_kernel(x_ref, nw_ref, wu_ref, wz5_ref, wzs_ref, wxbc_ref, wdt_ref, wg_ref,
                     convw_ref, convb_ref, dtb_ref, alog_ref, dskip_ref, snw_ref, wbs_ref,
                     u_ref, zs5_ref, g5_ref, mso_ref, mse_ref,
                     xraw0, dt0, zs0, gss0, xraw1, dt1, zs1, gss1, hb_ref, yn_ref, state_ref,
                     *, tiles_per_seq):
    m = pl.program_id(0)
    tl = dt0.shape[0]
    slots = ((xraw0, dt0, zs0, gss0), (xraw1, dt1, zs1, gss1))

    @pl.when(m == 0)
    def _():
        for ref in slots[0]:
            ref[...] = jnp.zeros(ref.shape, ref.dtype)
        state_ref[...] = jnp.zeros(state_ref.shape, F32)

    proj = functools.partial(_proj_items, x_ref, nw_ref, wu_ref, wz5_ref, wzs_ref, wxbc_ref, wdt_ref, wg_ref,
                             dtb_ref, u_ref, zs5_ref, g5_ref, hb_ref)
    ssd = functools.partial(_ssd_items, convw_ref=convw_ref, convb_ref=convb_ref, alog_ref=alog_ref,
                            dskip_ref=dskip_ref, snw_ref=snw_ref, wbs_ref=wbs_ref,
                            yn_ref=yn_ref, state_ref=state_ref)

    _run_interleaved([ssd(slots[0], slots[1], ms_ref=mso_ref), proj(rows=slice(0, tl), slot=slots[1])])

    @pl.when(lax.rem(2 * m, tiles_per_seq) == 0)
    def _():
        xraw1[:, 0:SUBLANES, :] = jnp.zeros((xraw1.shape[0], SUBLANES, LANES), F32)
        state_ref[...] = jnp.zeros(state_ref.shape, F32)

    _run_interleaved([ssd(slots[1], slots[0], ms_ref=mse_ref), proj(rows=slice(tl, 2 * tl), slot=slots[0])])


def _const_spec(shape):
    zeros = (0,) * len(shape)
    return pl.BlockSpec(shape, lambda *_: zeros)


def _mixer_in(x, norm_w, w_in, conv_w, conv_b, dt_bias, a_log, d_skip, ssd_norm_w, w_br_ssd):
    bsz, seq, d = x.shape
    tl = TL_IN
    assert seq % (2 * tl) == 0 and 2 * tl == TL_OUT
    rows_total = bsz * seq
    nsteps = rows_total // (2 * tl)
    sizes = (S5_WIDTH, S5_WIDTH, SSD_WIDTH, SSD_CONV_DIM, SSD_HEADS, N_BRANCH * D_MODEL)
    offs = [0]
    for s in sizes:
        offs.append(offs[-1] + s)
    wb = w_in.astype(BF16)
    wu, wz5, wzs, wxbc, wdt, wg = (wb[:, offs[i]:offs[i + 1]] for i in range(6))
    pad = LANES - SSD_HEADS
    wdt = jnp.pad(wdt, ((0, 0), (0, pad)))
    dtb = jnp.pad(dt_bias.reshape(1, SSD_HEADS), ((0, 0), (0, pad)))
    alog = jnp.pad(a_log.reshape(1, SSD_HEADS), ((0, 0), (0, pad)))
    dskip = jnp.repeat(d_skip, SSD_HEADDIM).reshape(1, SSD_WIDTH)
    consts = [norm_w.reshape(1, d), wu, wz5, wzs, wxbc, wdt, wg, conv_w, conv_b.reshape(1, -1), dtb, alog,
              dskip, ssd_norm_w.reshape(1, -1), w_br_ssd.astype(BF16)]
    proj_spec = lambda w: pl.BlockSpec((2 * tl, w), lambda m: (jnp.minimum(m, nsteps - 1), 0))
    slot = [pltpu.VMEM((SSD_CONV_DIM // LANES, tl + SUBLANES, LANES), F32), pltpu.VMEM((tl, LANES), F32),
            pltpu.VMEM((tl, SSD_WIDTH), F32), pltpu.VMEM((tl, D_MODEL), F32)]
    return pl.pallas_call(
        functools.partial(_mixer_in_kernel, tiles_per_seq=seq // tl),
        grid=(nsteps + 1,),
        in_specs=[proj_spec(d)] + [_const_spec(c.shape) for c in consts],
        out_specs=[proj_spec(S5_WIDTH), proj_spec(S5_WIDTH), proj_spec(D_MODEL),
                   pl.BlockSpec((tl, D_MODEL), lambda m: (jnp.maximum(m - 1, 0), 0)),
                   pl.BlockSpec((tl, D_MODEL), lambda m: (m, 0))],
        out_shape=[jax.ShapeDtypeStruct((rows_total, S5_WIDTH), F32),
                   jax.ShapeDtypeStruct((rows_total, S5_WIDTH), BF16),
                   jax.ShapeDtypeStruct((rows_total, D_MODEL), BF16),
                   jax.ShapeDtypeStruct((nsteps * tl, D_MODEL), BF16),
                   jax.ShapeDtypeStruct(((nsteps + 1) * tl, D_MODEL), BF16)],
        scratch_shapes=slot + slot + [pltpu.VMEM((tl, D_MODEL), BF16),
                                      pltpu.VMEM((tl, SSD_WIDTH), BF16),
                                      pltpu.VMEM((SSD_STATE, SSD_WIDTH), F32)],
        compiler_params=pltpu.CompilerParams(dimension_semantics=("arbitrary",),
                                             vmem_limit_bytes=VMEM_LIMIT_BYTES),
        name="mixer_in",
    )(x.reshape(rows_total, d), *consts)


def _s5_scan_kernel(u_ref, bh_ref, ch_ref, lam_ref, d_ref, wglu_ref, bglu_ref, y_ref,
                    st_ref, utm_ref, ytm_ref):
    bsz, steps, _ = u_ref.shape

    @pl.when(pl.program_id(0) == 0)
    def _():
        st_ref[...] = jnp.zeros(st_ref.shape, F32)

    nlt = utm_ref.shape[0]
    for b in range(bsz):
        for c in range(nlt):
            utm_ref[c, pl.ds(b, steps, stride=bsz), :] = u_ref[b, :, c * LANES:(c + 1) * LANES]
    u = jnp.concatenate([utm_ref[c] for c in range(nlt)], axis=1)
    ub = u.astype(BF16)
    ys = []
    for q in range(S5_HALVES):
        ubq = ub[:, q * S5_HCH:(q + 1) * S5_HCH]
        yq = None
        for k in range(S5_HST // S5_CW):
            cols = slice(2 * k * S5_CW, 2 * (k + 1) * S5_CW)
            bu = _dot(ubq, bh_ref[q, :, cols])
            lam = lam_ref[q:q + 1, cols]
            lr = jnp.broadcast_to(lam[:, :S5_CW], (bsz, S5_CW))
            li = jnp.broadcast_to(lam[:, S5_CW:], (bsz, S5_CW))
            st = st_ref[q, :, cols]
            sr, si = st[:, :S5_CW], st[:, S5_CW:]
            hist = []
            for t in range(steps):
                r = slice(t * bsz, (t + 1) * bsz)
                sr, si = (lr * sr - li * si + bu[r, :S5_CW],
                          lr * si + li * sr + bu[r, S5_CW:])
                hist.append(jnp.concatenate([sr, si], axis=1).astype(BF16))
            st_ref[q, :, cols] = jnp.concatenate([sr, si], axis=1)
            part = _dot(jnp.concatenate(hist, axis=0), ch_ref[q, cols, :])
            yq = part if yq is None else yq + part
        ys.append(yq)
    y = jax.nn.gelu(jnp.concatenate(ys, axis=1) + d_ref[...] * u)
    y = y * _sigmoid(_dot(y.astype(BF16), wglu_ref[...]) + bglu_ref[...])
    for c in range(nlt):
        ytm_ref[c] = y[:, c * LANES:(c + 1) * LANES]
    for b in range(bsz):
        for c in range(nlt):
            y_ref[b, :, c * LANES:(c + 1) * LANES] = ytm_ref[c, pl.ds(b, steps, stride=bsz), :].astype(BF16)


def _s5_scan(u, bh, ch, lam, s5_d, w_glu, b_glu):
    bsz, seq, _ = u.shape
    steps = TT_S5
    rows = steps * bsz
    consts = [bh, ch, lam, s5_d.reshape(1, -1), w_glu.astype(BF16), b_glu.reshape(1, -1)]
    return pl.pallas_call(
        _s5_scan_kernel,
        grid=(seq // steps,),
        in_specs=[pl.BlockSpec((bsz, steps, S5_WIDTH), lambda i: (0, i, 0))]
                 + [_const_spec(c.shape) for c in consts],
        out_specs=pl.BlockSpec((bsz, steps, S5_WIDTH), lambda i: (0, i, 0)),
        out_shape=jax.ShapeDtypeStruct((bsz, seq, S5_WIDTH), BF16),
        scratch_shapes=[pltpu.VMEM((S5_HALVES, bsz, 2 * S5_HST), F32),
                        pltpu.VMEM((S5_WIDTH // LANES, rows, LANES), F32),
                        pltpu.VMEM((S5_WIDTH // LANES, rows, LANES), F32)],
        compiler_params=pltpu.CompilerParams(dimension_semantics=("arbitrary",),
                                             vmem_limit_bytes=VMEM_LIMIT_BYTES),
        name="s5_scan",
    )(u, *consts)


def _merge_out_kernel(x_ref, p_ref, y5_ref, zs5_ref, g5_ref, mse_ref, mso_ref, wb5_ref, wout_ref, pnw_ref,
                      wpg_ref, wpp_ref, fnw_ref, o_ref):
    y5 = (y5_ref[...].astype(F32) * zs5_ref[...].astype(F32)).astype(BF16)
    ms = jnp.concatenate([mse_ref[...], mso_ref[...]], axis=0)
    merged = g5_ref[...].astype(F32) * _dot(y5, wb5_ref[...]) + ms.astype(F32)
    h = x_ref[...] + _dot(merged.astype(BF16), wout_ref[...])
    gate = _sigmoid(_dot(_rms(h, pnw_ref[...]).astype(BF16), wpg_ref[...]))
    h = h + gate * _dot(p_ref[...].astype(BF16), wpp_ref[...])
    o_ref[...] = _rms(h, fnw_ref[...])


def _merge_out(x, p, y5, zs5, g5, ms_even, ms_odd, w_br_s5, w_out, ple_norm_w, w_ple_gate, w_ple_proj,
               final_norm_w):
    bsz, seq, d = x.shape
    tl = TL_OUT
    consts = [w_br_s5.astype(BF16), w_out.astype(BF16), ple_norm_w.reshape(1, d), w_ple_gate.astype(BF16),
              w_ple_proj.astype(BF16), final_norm_w.reshape(1, d)]
    tok_spec = lambda w, rows=tl: pl.BlockSpec((rows, w), lambda b, t: (b * (seq // tl) + t, 0))
    return pl.pallas_call(
        _merge_out_kernel,
        grid=(bsz, seq // tl),
        in_specs=[pl.BlockSpec((None, tl, d), lambda b, t: (b, t, 0)),
                  pl.BlockSpec((None, tl, PLE_DIM), lambda b, t: (b, t, 0)),
                  tok_spec(S5_WIDTH), tok_spec(S5_WIDTH), tok_spec(D_MODEL),
                  tok_spec(D_MODEL, tl // 2), tok_spec(D_MODEL, tl // 2)]
                 + [_const_spec(c.shape) for c in consts],
        out_specs=pl.BlockSpec((None, tl, d), lambda b, t: (b, t, 0)),
        out_shape=jax.ShapeDtypeStruct((bsz, seq, d), F32),
        compiler_params=pltpu.CompilerParams(dimension_semantics=("parallel", "parallel"),
                                             vmem_limit_bytes=VMEM_LIMIT_BYTES),
        name="merge_out",
    )(x, p, y5, zs5, g5, ms_even, ms_odd, *consts)


def kernel(x, p, norm_w, w_in, s5_a_re, s5_a_im, s5_b_re, s5_b_im, s5_c_re, s5_c_im, s5_d, s5_log_step, s5_w_glu, s5_b_glu, ssd_conv_w, ssd_conv_b, ssd_dt_bias, ssd_a_log, ssd_d, ssd_norm_w, w_br_s5, w_br_ssd, w_out, ple_norm_w, w_ple_gate, w_ple_proj, final_norm_w):
    bsz, seq, _ = x.shape
    i = 0
    assert norm_w.shape[0] == 1

    lre, lim, bbre, bbim, ncim = _s5_discretize(s5_a_re[i], s5_a_im[i], s5_log_step[i], s5_b_re[i],
                                                s5_b_im[i], s5_c_im[i])
    c_re_t = jnp.transpose(s5_c_re[i], (1, 0, 2)).reshape(S5_GROUP, S5_NSTATE)
    bh = _s5_pair_blocks(_s5_block_diag(bbre), _s5_block_diag(bbim)).astype(BF16)
    ch = _s5_pair_blocks(_s5_block_diag(c_re_t), _s5_block_diag(ncim))
    ch = jnp.transpose(ch, (0, 2, 1)).astype(BF16)
    lam = _s5_pair_blocks(lre.reshape(S5_HALVES, 1, S5_HST), lim.reshape(S5_HALVES, 1, S5_HST))
    lam = lam.reshape(S5_HALVES, 2 * S5_HST)

    u, zs5, g5, ms_odd, ms_even = _mixer_in(x, norm_w[i], w_in[i], ssd_conv_w[i], ssd_conv_b[i],
                                            ssd_dt_bias[i], ssd_a_log[i], ssd_d[i], ssd_norm_w[i], w_br_ssd[i])
    y5 = _s5_scan(u.reshape(bsz, seq, S5_WIDTH), bh, ch, lam, s5_d[i], s5_w_glu[i], s5_b_glu[i])
    return _merge_out(x, p[i], y5.reshape(bsz * seq, S5_WIDTH), zs5, g5, ms_even, ms_odd, w_br_s5[i],
                      w_out[i], ple_norm_w[i], w_ple_gate[i], w_ple_proj[i], final_norm_w)
```

```python
import functools

import jax
import jax.numpy as jnp
from jax import lax
from jax.experimental import pallas as pl
from jax.experimental.pallas import tpu as pltpu

F32 = jnp.float32
BF16 = jnp.bfloat16

D_MODEL = 1024
PLE_DIM = 256
RMS_EPS = 1e-6
LOG2_E = 1.4426950408889634
S5_WIDTH = 512
S5_GROUP = 16
S5_GROUPS = S5_WIDTH // S5_GROUP
S5_STATE = 64
S5_NSTATE = S5_GROUPS * S5_STATE
S5_HALVES = 2
S5_HCH = S5_WIDTH // S5_HALVES
S5_HST = S5_NSTATE // S5_HALVES
SSD_WIDTH = 1536
SSD_HEADDIM = 64
SSD_HEADS = SSD_WIDTH // SSD_HEADDIM
SSD_GROUPS = 4
SSD_HPG = SSD_HEADS // SSD_GROUPS
SSD_STATE = 128
SSD_CONV = 4
SSD_BC = SSD_GROUPS * SSD_STATE
SSD_CONV_DIM = SSD_WIDTH + 2 * SSD_BC
SSD_GW = SSD_WIDTH // SSD_GROUPS
N_BRANCH = 2

MXU_COLS = 256
LANES = 128
SUBLANES = 8
VMEM_LIMIT_BYTES = 56 * 1024 * 1024

Q = 128
TL_IN = 256
TT_S5 = 32
S5_CW = 256
TL_OUT = 512


def _sigmoid(v):
    return jax.nn.sigmoid(v)


def _silu(v):
    return v * jax.nn.sigmoid(v)


def _softplus(v):
    return jnp.maximum(v, 0.0) + jnp.log1p(jnp.exp(-jnp.abs(v)))


def _rms(v, w):
    return v * lax.rsqrt(jnp.mean(v * v, axis=-1, keepdims=True) + RMS_EPS) * w


def _dot(a, b):
    return jnp.dot(a, b, preferred_element_type=F32)


def _s5_disc_kernel(are_ref, aim_ref, lstep_ref, bre_ref, bim_ref, cim_ref,
                    lre_ref, lim_ref, bbre_ref, bbim_ref, ncim_ref):
    a_re = are_ref[...]
    a_im = aim_ref[...]
    step = jnp.exp(lstep_ref[...])
    mag = jnp.exp(a_re * step)
    lb_re = mag * jnp.cos(a_im * step)
    lb_im = mag * jnp.sin(a_im * step)
    den = a_re * a_re + a_im * a_im
    n_re = lb_re - 1.0
    n_im = lb_im
    f_re = (n_re * a_re + n_im * a_im) / den
    f_im = (n_im * a_re - n_re * a_im) / den
    b_re = bre_ref[...]
    b_im = bim_ref[...]
    lre_ref[...] = lb_re
    lim_ref[...] = lb_im
    bbre_ref[...] = f_re * b_re - f_im * b_im
    bbim_ref[...] = f_re * b_im + f_im * b_re
    ncim_ref[...] = -cim_ref[...]


def _s5_discretize(a_re, a_im, log_step, b_re, b_im, c_im):
    n = S5_NSTATE
    row = lambda v: v.reshape(1, n)
    b_t = lambda v: jnp.transpose(v, (2, 0, 1)).reshape(S5_GROUP, n)
    c_t = lambda v: jnp.transpose(v, (1, 0, 2)).reshape(S5_GROUP, n)
    lstep = jnp.repeat(log_step, S5_STATE).reshape(1, n)
    outs = pl.pallas_call(
        _s5_disc_kernel,
        out_shape=[jax.ShapeDtypeStruct((1, n), F32), jax.ShapeDtypeStruct((1, n), F32),
                   jax.ShapeDtypeStruct((S5_GROUP, n), F32), jax.ShapeDtypeStruct((S5_GROUP, n), F32),
                   jax.ShapeDtypeStruct((S5_GROUP, n), F32)],
        name="s5_discretize",
    )(row(a_re), row(a_im), lstep, b_t(b_re), b_t(b_im), c_t(c_im))
    return outs


def _s5_block_diag(m_t):
    gl = S5_GROUPS // S5_HALVES
    m4 = m_t.reshape(S5_GROUP, S5_HALVES, gl, S5_STATE)
    eye = jnp.eye(gl, dtype=m_t.dtype)
    out = jnp.einsum('hqgp,gk->qghkp', m4, eye)
    return out.reshape(S5_HALVES, S5_HCH, S5_HST)


def _s5_pair_blocks(re, im):
    h, r, _ = re.shape
    nb = S5_HST // S5_CW
    both = jnp.stack([re.reshape(h, r, nb, S5_CW), im.reshape(h, r, nb, S5_CW)], axis=3)
    return both.reshape(h, r, 2 * S5_HST)


def _run_interleaved(streams):
    acc = [0.0] * len(streams)
    live = list(range(len(streams)))
    while live:
        i = min(live, key=lambda j: acc[j])
        try:
            acc[i] += next(streams[i])
        except StopIteration:
            live.remove(i)


def _proj_items(x_ref, nw_ref, wmain_ref, wtail_ref, dtb_ref, u_ref, zs5_ref, g5_ref, hb_ref, rows, slot):
    xraw_ref, dt_ref, zs_ref, gss_ref = slot
    tl = hb_ref.shape[0]
    hb_ref[...] = _rms(x_ref[rows, :], nw_ref[...]).astype(BF16)
    yield 300

    def blocks(w_ref, start, n):
        for c0 in range(0, n, MXU_COLS):
            c1 = min(c0 + MXU_COLS, n)
            yield slice(c0, c1), _dot(hb_ref[...], w_ref[:, start + c0:start + c1])

    cost = tl
    z5_off = S5_WIDTH
    zs_off = 2 * S5_WIDTH
    xbc_off = zs_off + SSD_WIDTH
    for cs, r in blocks(wmain_ref, xbc_off, SSD_CONV_DIM):
        for j in range(0, cs.stop - cs.start, LANES):
            xraw_ref[(cs.start + j) // LANES, SUBLANES:SUBLANES + tl, :] = r[:, j:j + LANES]
        yield cost
    for cs, r in blocks(wtail_ref, 0, LANES):
        dt_ref[:, cs] = _softplus(r + dtb_ref[:, cs])
        yield cost
    for cs, r in blocks(wmain_ref, 0, S5_WIDTH):
        u_ref[rows, cs] = r
        yield cost
    for cs, r in blocks(wmain_ref, z5_off, S5_WIDTH):
        zs5_ref[rows, cs] = _silu(r).astype(BF16)
        yield cost
    for cs, r in blocks(wtail_ref, LANES, N_BRANCH * D_MODEL):
        g = _sigmoid(r)
        if cs.start < D_MODEL:
            g5_ref[rows, cs] = g.astype(BF16)
        else:
            gss_ref[:, cs.start - D_MODEL:cs.stop - D_MODEL] = g
        yield cost
    for cs, r in blocks(wmain_ref, zs_off, SSD_WIDTH):
        zs_ref[:, cs] = r
        yield cost


def _cumsum_rows(tril_b, v):
    hi = v.astype(BF16)
    r1 = v - hi.astype(F32)
    mid = r1.astype(BF16)
    lo = (r1 - mid.astype(F32)).astype(BF16)
    w = v.shape[1]
    s = _dot(tril_b, jnp.concatenate([hi, mid, lo], axis=1))
    return s[:, :w] + s[:, w:2 * w] + s[:, 2 * w:]


def _ssd_items(slot, next_slot, convw_ref, convb_ref, alog_ref, dskip_ref, snw_ref, wbs_ref,
               ms_ref, yn_ref, state_ref):
    xraw_ref, dt_ref, zs_ref, gss_ref = slot
    tl = dt_ref.shape[0]
    tail = SUBLANES

    def conv_tile(ct, r0):
        cs = slice(ct * LANES, (ct + 1) * LANES)
        acc = convb_ref[:, cs]
        for k in range(SSD_CONV):
            off = r0 + tail - (SSD_CONV - 1) + k
            acc = acc + convw_ref[k:k + 1, cs] * xraw_ref[ct, off:off + Q, :]
        return _silu(acc)

    a_row = -jnp.exp(alog_ref[...])
    ri = lax.broadcasted_iota(jnp.int32, (Q, Q), 0)
    ci = lax.broadcasted_iota(jnp.int32, (Q, Q), 1)
    causal = ci <= ri
    tril_b = causal.astype(BF16)
    lane_lo = lax.broadcasted_iota(jnp.int32, (Q, LANES), 1) < SSD_HEADDIM
    b_tile = SSD_WIDTH // LANES
    c_tile = (SSD_WIDTH + SSD_BC) // LANES

    for c in range(tl // Q):
        r0 = c * Q
        rows = slice(r0, r0 + Q)
        dt = dt_ref[rows, :]
        a_cum = _cumsum_rows(tril_b, dt * a_row) * LOG2_E
        a_cum_t = a_cum.T
        row_t = a_cum_t - jnp.log2(dt.T)
        ea_all = jnp.exp2(a_cum)
        wcol_all = jnp.exp2(a_cum[Q - 1:Q, :] - a_cum) * dt
        yield 100
        for g in range(SSD_GROUPS):
            bg = conv_tile(b_tile + g, r0).astype(BF16)
            cg = conv_tile(c_tile + g, r0).astype(BF16)
            scores = lax.dot_general(cg, bg, (((1,), (1,)), ((), ())), preferred_element_type=F32)
            gcols = slice(g * SSD_GW, (g + 1) * SSD_GW)
            y_off = _dot(cg, state_ref[:, gcols].astype(BF16))
            yield 150
            xw_parts = []
            decay_parts = []
            y_parts = []
            for k in range(SSD_HPG // 2):
                cols = slice(g * SSD_GW + k * LANES, g * SSD_GW + (k + 1) * LANES)
                xs = conv_tile(cols.start // LANES, r0)
                wts, ea, wcol = [], [], []
                for h in (g * SSD_HPG + 2 * k, g * SSD_HPG + 2 * k + 1):
                    seg = jnp.broadcast_to(a_cum[:, h:h + 1], (Q, Q)) - row_t[h:h + 1, :]
                    lmat = jnp.exp2(jnp.where(causal, seg, -jnp.inf))
                    wts.append((scores * lmat).astype(BF16))
                    ea.append(jnp.broadcast_to(ea_all[:, h:h + 1], (Q, LANES)))
                    wcol.append(jnp.broadcast_to(wcol_all[:, h:h + 1], (Q, LANES)))
                xb = xs.astype(BF16)
                zero = jnp.zeros_like(xb)
                rhs = jnp.concatenate([jnp.where(lane_lo, xb, zero), jnp.where(lane_lo, zero, xb)], axis=0)
                y_diag = _dot(jnp.concatenate(wts, axis=1), rhs)
                ea2 = jnp.where(lane_lo, ea[0], ea[1])
                wcol2 = jnp.where(lane_lo, wcol[0], wcol[1])
                y_parts.append(y_diag + y_off[:, k * LANES:(k + 1) * LANES] * ea2 + xs * dskip_ref[:, cols])
                xw_parts.append((xs * wcol2).astype(BF16))
                decay_parts.append(ea2[Q - 1:Q, :])
                yield 250
            xw = jnp.concatenate(xw_parts, axis=1)
            new = lax.dot_general(bg, xw, (((0,), (0,)), ((), ())), preferred_element_type=F32)
            decay = jnp.concatenate(decay_parts, axis=1)
            state_ref[:, gcols] = state_ref[:, gcols] * decay + new
            yz = jnp.concatenate(y_parts, axis=1) * _silu(zs_ref[rows, gcols])
            yn_ref[rows, gcols] = _rms(yz, snw_ref[:, gcols]).astype(BF16)
            yield 250
    next_slot[0][:, 0:tail, :] = xraw_ref[:, tl:tl + tail, :]

    for c0 in range(0, D_MODEL, MXU_COLS):
        cs = slice(c0, c0 + MXU_COLS)
        ms_ref[:, cs] = (gss_ref[:, cs] * _dot(yn_ref[...], wbs_ref[:, cs])).astype(BF16)
        yield 400


def _mixer_in_kernel(x_ref, nw_ref, wmain_ref, wtail_ref,
                     convw_ref, convb_ref, dtb_ref, alog_ref, dskip_ref, snw_ref, wbs_ref,
                     u_ref, zs5_ref, g5_ref, mso_ref, mse_ref,
                     xraw0, dt0, zs0, gss0, xraw1, dt1, zs1, gss1, hb_ref, yn_ref, state_ref,
                     *, tiles_per_seq):
    m = pl.program_id(0)
    tl = dt0.shape[0]
    slots = ((xraw0, dt0, zs0, gss0), (xraw1, dt1, zs1, gss1))

    @pl.when(m == 0)
    def _():
        for ref in slots[0]:
            ref[...] = jnp.zeros(ref.shape, ref.dtype)
        state_ref[...] = jnp.zeros(state_ref.shape, F32)

    proj = functools.partial(_proj_items, x_ref, nw_ref, wmain_ref, wtail_ref, dtb_ref, u_ref, zs5_ref, g5_ref,
                             hb_ref)
    ssd = functools.partial(_ssd_items, convw_ref=convw_ref, convb_ref=convb_ref, alog_ref=alog_ref,
                            dskip_ref=dskip_ref, snw_ref=snw_ref, wbs_ref=wbs_ref,
                            yn_ref=yn_ref, state_ref=state_ref)

    _run_interleaved([ssd(slots[0], slots[1], ms_ref=mso_ref), proj(rows=slice(0, tl), slot=slots[1])])

    @pl.when(lax.rem(2 * m, tiles_per_seq) == 0)
    def _():
        xraw1[:, 0:SUBLANES, :] = jnp.zeros((xraw1.shape[0], SUBLANES, LANES), F32)
        state_ref[...] = jnp.zeros(state_ref.shape, F32)

    _run_interleaved([ssd(slots[1], slots[0], ms_ref=mse_ref), proj(rows=slice(tl, 2 * tl), slot=slots[0])])


def _const_spec(shape):
    zeros = (0,) * len(shape)
    return pl.BlockSpec(shape, lambda *_: zeros)


def _mixer_in(x, norm_w, w_in, conv_w, conv_b, dt_bias, a_log, d_skip, ssd_norm_w, w_br_ssd):
    bsz, seq, d = x.shape
    tl = TL_IN
    assert seq % (2 * tl) == 0 and 2 * tl == TL_OUT
    rows_total = bsz * seq
    nsteps = rows_total // (2 * tl)
    n_main = 2 * S5_WIDTH + SSD_WIDTH + SSD_CONV_DIM
    pad = LANES - SSD_HEADS
    w_main = w_in[:, :n_main].astype(BF16)
    w_tail = jnp.concatenate([w_in[:, n_main:n_main + SSD_HEADS], jnp.zeros((d, pad), w_in.dtype),
                              w_in[:, n_main + SSD_HEADS:]], axis=1).astype(BF16)
    dtb = jnp.pad(dt_bias.reshape(1, SSD_HEADS), ((0, 0), (0, pad)))
    alog = jnp.pad(a_log.reshape(1, SSD_HEADS), ((0, 0), (0, pad)))
    dskip = jnp.repeat(d_skip, SSD_HEADDIM).reshape(1, SSD_WIDTH)
    consts = [norm_w.reshape(1, d), w_main, w_tail, conv_w, conv_b.reshape(1, -1), dtb, alog,
              dskip, ssd_norm_w.reshape(1, -1), w_br_ssd.astype(BF16)]
    spb = seq // (2 * tl)
    proj_spec = lambda w: pl.BlockSpec((2 * tl, w), lambda m: (jnp.minimum(m, nsteps - 1), 0))

    def seq_spec(w):
        def index(m):
            mm = jnp.minimum(m, nsteps - 1)
            return mm // spb, mm % spb, 0
        return pl.BlockSpec((None, 2 * tl, w), index)
    slot = [pltpu.VMEM((SSD_CONV_DIM // LANES, tl + SUBLANES, LANES), F32), pltpu.VMEM((tl, LANES), F32),
            pltpu.VMEM((tl, SSD_WIDTH), F32), pltpu.VMEM((tl, D_MODEL), F32)]
    return pl.pallas_call(
        functools.partial(_mixer_in_kernel, tiles_per_seq=seq // tl),
        grid=(nsteps + 1,),
        in_specs=[seq_spec(d)] + [_const_spec(c.shape) for c in consts],
        out_specs=[seq_spec(S5_WIDTH), proj_spec(S5_WIDTH), proj_spec(D_MODEL),
                   pl.BlockSpec((tl, D_MODEL), lambda m: (jnp.maximum(m - 1, 0), 0)),
                   pl.BlockSpec((tl, D_MODEL), lambda m: (m, 0))],
        out_shape=[jax.ShapeDtypeStruct((bsz, seq, S5_WIDTH), F32),
                   jax.ShapeDtypeStruct((rows_total, S5_WIDTH), BF16),
                   jax.ShapeDtypeStruct((rows_total, D_MODEL), BF16),
                   jax.ShapeDtypeStruct((nsteps * tl, D_MODEL), BF16),
                   jax.ShapeDtypeStruct(((nsteps + 1) * tl, D_MODEL), BF16)],
        scratch_shapes=slot + slot + [pltpu.VMEM((tl, D_MODEL), BF16),
                                      pltpu.VMEM((tl, SSD_WIDTH), BF16),
                                      pltpu.VMEM((SSD_STATE, SSD_WIDTH), F32)],
        compiler_params=pltpu.CompilerParams(dimension_semantics=("arbitrary",),
                                             vmem_limit_bytes=VMEM_LIMIT_BYTES),
        name="mixer_in",
    )(x, *consts)


def _s5_scan_kernel(u_ref, bh_ref, ch_ref, lam_ref, d_ref, wglu_ref, bglu_ref, y_ref,
                    st_ref, utm_ref, ytm_ref):
    bsz, steps, _ = u_ref.shape

    @pl.when(pl.program_id(0) == 0)
    def _():
        st_ref[...] = jnp.zeros(st_ref.shape, F32)

    nlt = utm_ref.shape[0]
    for b in range(bsz):
        for c in range(nlt):
            utm_ref[c, pl.ds(b, steps, stride=bsz), :] = u_ref[b, :, c * LANES:(c + 1) * LANES]
    u = jnp.concatenate([utm_ref[c] for c in range(nlt)], axis=1)
    ub = u.astype(BF16)
    ys = []
    for q in range(S5_HALVES):
        ubq = ub[:, q * S5_HCH:(q + 1) * S5_HCH]
        yq = None
        for k in range(S5_HST // S5_CW):
            cols = slice(2 * k * S5_CW, 2 * (k + 1) * S5_CW)
            bu = _dot(ubq, bh_ref[q, :, cols])
            lam = lam_ref[q:q + 1, cols]
            lr = jnp.broadcast_to(lam[:, :S5_CW], (bsz, S5_CW))
            li = jnp.broadcast_to(lam[:, S5_CW:], (bsz, S5_CW))
            st = st_ref[q, :, cols]
            sr, si = st[:, :S5_CW], st[:, S5_CW:]
            hist = []
            for t in range(steps):
                r = slice(t * bsz, (t + 1) * bsz)
                sr, si = (lr * sr - li * si + bu[r, :S5_CW],
                          lr * si + li * sr + bu[r, S5_CW:])
                hist.append(jnp.concatenate([sr, si], axis=1).astype(BF16))
            st_ref[q, :, cols] = jnp.concatenate([sr, si], axis=1)
            part = _dot(jnp.concatenate(hist, axis=0), ch_ref[q, cols, :])
            yq = part if yq is None else yq + part
        ys.append(yq)
    y = jax.nn.gelu(jnp.concatenate(ys, axis=1) + d_ref[...] * u)
    y = y * _sigmoid(_dot(y.astype(BF16), wglu_ref[...]) + bglu_ref[...])
    for c in range(nlt):
        ytm_ref[c] = y[:, c * LANES:(c + 1) * LANES]
    for b in range(bsz):
        for c in range(nlt):
            y_ref[b, :, c * LANES:(c + 1) * LANES] = ytm_ref[c, pl.ds(b, steps, stride=bsz), :].astype(BF16)


def _s5_scan(u, bh, ch, lam, s5_d, w_glu, b_glu):
    bsz, seq, _ = u.shape
    steps = TT_S5
    rows = steps * bsz
    consts = [bh, ch, lam, s5_d.reshape(1, -1), w_glu.astype(BF16), b_glu.reshape(1, -1)]
    return pl.pallas_call(
        _s5_scan_kernel,
        grid=(seq // steps,),
        in_specs=[pl.BlockSpec((bsz, steps, S5_WIDTH), lambda i: (0, i, 0))]
                 + [_const_spec(c.shape) for c in consts],
        out_specs=pl.BlockSpec((bsz, steps, S5_WIDTH), lambda i: (0, i, 0)),
        out_shape=jax.ShapeDtypeStruct((bsz, seq, S5_WIDTH), BF16),
        scratch_shapes=[pltpu.VMEM((S5_HALVES, bsz, 2 * S5_HST), F32),
                        pltpu.VMEM((S5_WIDTH // LANES, rows, LANES), F32),
                        pltpu.VMEM((S5_WIDTH // LANES, rows, LANES), F32)],
        compiler_params=pltpu.CompilerParams(dimension_semantics=("arbitrary",),
                                             vmem_limit_bytes=VMEM_LIMIT_BYTES),
        name="s5_scan",
    )(u, *consts)


def _merge_rows(x_ref, p_ref, y5_ref, zs5_ref, g5_ref, ms_ref, wb5_ref, wout_ref, pnw_ref, wpg_ref, wpp_ref,
                fnw_ref, o_ref, rows):
    y5 = (y5_ref[rows, :].astype(F32) * zs5_ref[rows, :].astype(F32)).astype(BF16)
    yield 1
    branch = _dot(y5, wb5_ref[...])
    yield 1
    merged = (g5_ref[rows, :].astype(F32) * branch + ms_ref[...].astype(F32)).astype(BF16)
    yield 1
    h = x_ref[rows, :] + _dot(merged, wout_ref[...])
    yield 1
    hn = _rms(h, pnw_ref[...]).astype(BF16)
    yield 1
    gate = _sigmoid(_dot(hn, wpg_ref[...]))
    yield 1
    h = h + gate * _dot(p_ref[rows, :].astype(BF16), wpp_ref[...])
    yield 1
    o_ref[rows, :] = _rms(h, fnw_ref[...])
    yield 1


def _merge_out_kernel(x_ref, p_ref, y5_ref, zs5_ref, g5_ref, mse_ref, mso_ref, wb5_ref, wout_ref, pnw_ref,
                      wpg_ref, wpp_ref, fnw_ref, o_ref):
    half = mse_ref.shape[0]
    _run_interleaved([
        _merge_rows(x_ref, p_ref, y5_ref, zs5_ref, g5_ref, ms_ref, wb5_ref, wout_ref, pnw_ref, wpg_ref, wpp_ref,
                    fnw_ref, o_ref, slice(j * half, (j + 1) * half))
        for j, ms_ref in enumerate((mse_ref, mso_ref))])


def _merge_out(x, p, y5, zs5, g5, ms_even, ms_odd, w_br_s5, w_out, ple_norm_w, w_ple_gate, w_ple_proj,
               final_norm_w):
    bsz, seq, d = x.shape
    tl = TL_OUT
    consts = [w_br_s5.astype(BF16), w_out.astype(BF16), ple_norm_w.reshape(1, d), w_ple_gate.astype(BF16),
              w_ple_proj.astype(BF16), final_norm_w.reshape(1, d)]
    tok_spec = lambda w, rows=tl: pl.BlockSpec((rows, w), lambda b, t: (b * (seq // tl) + t, 0))
    return pl.pallas_call(
        _merge_out_kernel,
        grid=(bsz, seq // tl),
        in_specs=[pl.BlockSpec((None, tl, d), lambda b, t: (b, t, 0)),
                  pl.BlockSpec((None, tl, PLE_DIM), lambda b, t: (b, t, 0)),
                  pl.BlockSpec((None, tl, S5_WIDTH), lambda b, t: (b, t, 0)),
                  tok_spec(S5_WIDTH), tok_spec(D_MODEL),
                  tok_spec(D_MODEL, tl // 2), tok_spec(D_MODEL, tl // 2)]
                 + [_const_spec(c.shape) for c in consts],
        out_specs=pl.BlockSpec((None, tl, d), lambda b, t: (b, t, 0)),
        out_shape=jax.ShapeDtypeStruct((bsz, seq, d), F32),
        compiler_params=pltpu.CompilerParams(dimension_semantics=("parallel", "parallel"),
                                             vmem_limit_bytes=VMEM_LIMIT_BYTES),
        name="merge_out",
    )(x, p, y5, zs5, g5, ms_even, ms_odd, *consts)


def kernel(x, p, norm_w, w_in, s5_a_re, s5_a_im, s5_b_re, s5_b_im, s5_c_re, s5_c_im, s5_d, s5_log_step, s5_w_glu, s5_b_glu, ssd_conv_w, ssd_conv_b, ssd_dt_bias, ssd_a_log, ssd_d, ssd_norm_w, w_br_s5, w_br_ssd, w_out, ple_norm_w, w_ple_gate, w_ple_proj, final_norm_w):
    bsz, seq, _ = x.shape
    i = 0
    assert norm_w.shape[0] == 1

    lre, lim, bbre, bbim, ncim = _s5_discretize(s5_a_re[i], s5_a_im[i], s5_log_step[i], s5_b_re[i],
                                                s5_b_im[i], s5_c_im[i])
    c_re_t = jnp.transpose(s5_c_re[i], (1, 0, 2)).reshape(S5_GROUP, S5_NSTATE)
    bh = _s5_pair_blocks(_s5_block_diag(bbre), _s5_block_diag(bbim)).astype(BF16)
    ch = _s5_pair_blocks(_s5_block_diag(c_re_t), _s5_block_diag(ncim))
    ch = jnp.transpose(ch, (0, 2, 1)).astype(BF16)
    lam = _s5_pair_blocks(lre.reshape(S5_HALVES, 1, S5_HST), lim.reshape(S5_HALVES, 1, S5_HST))
    lam = lam.reshape(S5_HALVES, 2 * S5_HST)

    u, zs5, g5, ms_odd, ms_even = _mixer_in(x, norm_w[i], w_in[i], ssd_conv_w[i], ssd_conv_b[i],
                                            ssd_dt_bias[i], ssd_a_log[i], ssd_d[i], ssd_norm_w[i], w_br_ssd[i])
    y5 = _s5_scan(u, bh, ch, lam, s5_d[i], s5_w_glu[i], s5_b_glu[i])
    return _merge_out(x, p[i], y5, zs5, g5, ms_even, ms_odd, w_br_s5[i], w_out[i], ple_norm_w[i],
                      w_ple_gate[i], w_ple_proj[i], final_norm_w)
```

```python
import functools

import jax
import jax.numpy as jnp
from jax import lax
from jax.experimental import pallas as pl
from jax.experimental.pallas import tpu as pltpu

F32 = jnp.float32
BF16 = jnp.bfloat16

D_MODEL = 1024
PLE_DIM = 256
RMS_EPS = 1e-6
LOG2_E = 1.4426950408889634
S5_WIDTH = 512
S5_GROUP = 16
S5_GROUPS = S5_WIDTH // S5_GROUP
S5_STATE = 64
S5_NSTATE = S5_GROUPS * S5_STATE
S5_HALVES = 2
S5_HCH = S5_WIDTH // S5_HALVES
S5_HST = S5_NSTATE // S5_HALVES
SSD_WIDTH = 1536
SSD_HEADDIM = 64
SSD_HEADS = SSD_WIDTH // SSD_HEADDIM
SSD_GROUPS = 4
SSD_HPG = SSD_HEADS // SSD_GROUPS
SSD_STATE = 128
SSD_CONV = 4
SSD_BC = SSD_GROUPS * SSD_STATE
SSD_CONV_DIM = SSD_WIDTH + 2 * SSD_BC
SSD_GW = SSD_WIDTH // SSD_GROUPS
N_BRANCH = 2

MXU_COLS = 256
LANES = 128
SUBLANES = 8
VMEM_LIMIT_BYTES = 56 * 1024 * 1024

Q = 128
TL_IN = 256
TT_S5 = 64
S5_CW = 256
TL_OUT = 4 * TL_IN


def _sigmoid(v):
    return jax.nn.sigmoid(v)


def _silu(v):
    return v * jax.nn.sigmoid(v)


def _softplus(v):
    return jnp.maximum(v, 0.0) + jnp.log1p(jnp.exp(-jnp.abs(v)))


def _rms(v, w):
    return v * lax.rsqrt(jnp.mean(v * v, axis=-1, keepdims=True) + RMS_EPS) * w


def _dot(a, b):
    return jnp.dot(a, b, preferred_element_type=F32)


def _s5_pair_blocks(re, im):
    blocks = []
    for k in range(S5_HST // S5_CW):
        blocks += [v[:, k * S5_CW:(k + 1) * S5_CW] for v in (re, im)]
    return jnp.concatenate(blocks, axis=1)


def _s5_half_maps(re_t, im_t, q):
    half = slice(q * S5_HST, (q + 1) * S5_HST)
    reps = S5_HCH // S5_GROUP
    row = lax.broadcasted_iota(jnp.int32, (S5_HCH, S5_HST), 0)
    col = lax.broadcasted_iota(jnp.int32, (S5_HCH, S5_HST), 1)
    own = (row // S5_GROUP) == (col // S5_STATE)
    re, im = (jnp.where(own, jnp.concatenate([v[:, half]] * reps, axis=0), 0.0) for v in (re_t, im_t))
    return _s5_pair_blocks(re, im)


def _s5_disc_kernel(are_ref, aim_ref, lstep_ref, bre_ref, bim_ref, cre_ref, cim_ref,
                    lam_ref, bh_ref, ch_ref):
    a_re = are_ref[...]
    a_im = aim_ref[...]
    step = jnp.exp(lstep_ref[...])
    mag = jnp.exp(a_re * step)
    lb_re = mag * jnp.cos(a_im * step)
    lb_im = mag * jnp.sin(a_im * step)
    den = a_re * a_re + a_im * a_im
    n_re = lb_re - 1.0
    n_im = lb_im
    f_re = (n_re * a_re + n_im * a_im) / den
    f_im = (n_im * a_re - n_re * a_im) / den
    b_re = bre_ref[...]
    b_im = bim_ref[...]
    bb_re = f_re * b_re - f_im * b_im
    bb_im = f_re * b_im + f_im * b_re
    c_re = cre_ref[...]
    nc_im = -cim_ref[...]
    for q in range(S5_HALVES):
        bh_ref[q] = _s5_half_maps(bb_re, bb_im, q).astype(BF16)
        ch_ref[q] = _s5_half_maps(c_re, nc_im, q).T.astype(BF16)
        half = slice(q * S5_HST, (q + 1) * S5_HST)
        lam_ref[q:q + 1, :] = _s5_pair_blocks(lb_re[:, half], lb_im[:, half])


def _s5_discretize(a_re, a_im, log_step, b_re, b_im, c_re, c_im):
    n = S5_NSTATE
    row = lambda v: v.reshape(1, n)
    b_t = lambda v: jnp.transpose(v, (2, 0, 1)).reshape(S5_GROUP, n)
    c_t = lambda v: jnp.transpose(v, (1, 0, 2)).reshape(S5_GROUP, n)
    lstep = jnp.repeat(log_step, S5_STATE).reshape(1, n)
    return pl.pallas_call(
        _s5_disc_kernel,
        out_shape=[jax.ShapeDtypeStruct((S5_HALVES, 2 * S5_HST), F32),
                   jax.ShapeDtypeStruct((S5_HALVES, S5_HCH, 2 * S5_HST), BF16),
                   jax.ShapeDtypeStruct((S5_HALVES, 2 * S5_HST, S5_HCH), BF16)],
        name="s5_discretize",
    )(row(a_re), row(a_im), lstep, b_t(b_re), b_t(b_im), c_t(c_re), c_t(c_im))


def _run_interleaved(streams):
    acc = [0.0] * len(streams)
    live = list(range(len(streams)))
    while live:
        i = min(live, key=lambda j: acc[j])
        try:
            acc[i] += next(streams[i])
        except StopIteration:
            live.remove(i)


def _proj_items(x_ref, nw_ref, wmain_ref, wtail_ref, dtb_ref, u_ref, zs5_ref, g5_ref, hb_ref, rows, slot):
    xraw_ref, dt_ref, zs_ref, gss_ref = slot
    tl = hb_ref.shape[0]
    hb_ref[...] = _rms(x_ref[rows, :], nw_ref[...]).astype(BF16)
    yield 300

    def blocks(w_ref, start, n):
        for c0 in range(0, n, MXU_COLS):
            c1 = min(c0 + MXU_COLS, n)
            yield slice(c0, c1), _dot(hb_ref[...], w_ref[:, start + c0:start + c1])

    cost = tl
    z5_off = S5_WIDTH
    zs_off = 2 * S5_WIDTH
    xbc_off = zs_off + SSD_WIDTH
    for cs, r in blocks(wmain_ref, xbc_off, SSD_CONV_DIM):
        for j in range(0, cs.stop - cs.start, LANES):
            xraw_ref[(cs.start + j) // LANES, SUBLANES:SUBLANES + tl, :] = r[:, j:j + LANES]
        yield cost
    for cs, r in blocks(wtail_ref, 0, LANES):
        dt_ref[:, cs] = _softplus(r + dtb_ref[:, cs])
        yield cost
    for cs, r in blocks(wmain_ref, 0, S5_WIDTH):
        u_ref[rows, cs] = r
        yield cost
    for cs, r in blocks(wmain_ref, z5_off, S5_WIDTH):
        zs5_ref[rows, cs] = _silu(r).astype(BF16)
        yield cost
    for cs, r in blocks(wtail_ref, LANES, N_BRANCH * D_MODEL):
        g = _sigmoid(r)
        if cs.start < D_MODEL:
            g5_ref[rows, cs] = g.astype(BF16)
        else:
            gss_ref[:, cs.start - D_MODEL:cs.stop - D_MODEL] = g
        yield cost
    for cs, r in blocks(wmain_ref, zs_off, SSD_WIDTH):
        zs_ref[:, cs] = r
        yield cost


def _cumsum_rows(tril_b, v):
    hi = v.astype(BF16)
    r1 = v - hi.astype(F32)
    mid = r1.astype(BF16)
    lo = (r1 - mid.astype(F32)).astype(BF16)
    w = v.shape[1]
    s = _dot(tril_b, jnp.concatenate([hi, mid, lo], axis=1))
    return s[:, :w] + s[:, w:2 * w] + s[:, 2 * w:]


def _ssd_items(slot, next_slot, convw_ref, convb_ref, alog_ref, dskip_ref, snw_ref, wbs_ref,
               ms_ref, yn_ref, state_ref):
    xraw_ref, dt_ref, zs_ref, gss_ref = slot
    tl = dt_ref.shape[0]
    tail = SUBLANES

    def conv_tile(ct, r0):
        cs = slice(ct * LANES, (ct + 1) * LANES)
        acc = convb_ref[:, cs]
        for k in range(SSD_CONV):
            off = r0 + tail - (SSD_CONV - 1) + k
            acc = acc + convw_ref[k:k + 1, cs] * xraw_ref[ct, off:off + Q, :]
        return _silu(acc)

    a_row = -jnp.exp(alog_ref[...])
    ri = lax.broadcasted_iota(jnp.int32, (Q, Q), 0)
    ci = lax.broadcasted_iota(jnp.int32, (Q, Q), 1)
    causal = ci <= ri
    tril_b = causal.astype(BF16)
    lane_lo = lax.broadcasted_iota(jnp.int32, (Q, LANES), 1) < SSD_HEADDIM
    b_tile = SSD_WIDTH // LANES
    c_tile = (SSD_WIDTH + SSD_BC) // LANES

    for c in range(tl // Q):
        r0 = c * Q
        rows = slice(r0, r0 + Q)
        dt = dt_ref[rows, :]
        a_cum = _cumsum_rows(tril_b, dt * a_row) * LOG2_E
        a_cum_t = a_cum.T
        row_t = a_cum_t - jnp.log2(dt.T)
        ea_all = jnp.exp2(a_cum)
        wcol_all = jnp.exp2(a_cum[Q - 1:Q, :] - a_cum) * dt
        yield 100
        for g in range(SSD_GROUPS):
            bg = conv_tile(b_tile + g, r0).astype(BF16)
            cg = conv_tile(c_tile + g, r0).astype(BF16)
            scores = lax.dot_general(cg, bg, (((1,), (1,)), ((), ())), preferred_element_type=F32)
            gcols = slice(g * SSD_GW, (g + 1) * SSD_GW)
            y_off = _dot(cg, state_ref[:, gcols].astype(BF16))
            yield 150
            xw_parts = []
            decay_parts = []
            y_parts = []
            for k in range(SSD_HPG // 2):
                cols = slice(g * SSD_GW + k * LANES, g * SSD_GW + (k + 1) * LANES)
                xs = conv_tile(cols.start // LANES, r0)
                wts, ea, wcol = [], [], []
                for h in (g * SSD_HPG + 2 * k, g * SSD_HPG + 2 * k + 1):
                    seg = jnp.broadcast_to(a_cum[:, h:h + 1], (Q, Q)) - row_t[h:h + 1, :]
                    lmat = jnp.exp2(jnp.where(causal, seg, -jnp.inf))
                    wts.append((scores * lmat).astype(BF16))
                    ea.append(jnp.broadcast_to(ea_all[:, h:h + 1], (Q, LANES)))
                    wcol.append(jnp.broadcast_to(wcol_all[:, h:h + 1], (Q, LANES)))
                xb = xs.astype(BF16)
                zero = jnp.zeros_like(xb)
                rhs = jnp.concatenate([jnp.where(lane_lo, xb, zero), jnp.where(lane_lo, zero, xb)], axis=0)
                y_diag = _dot(jnp.concatenate(wts, axis=1), rhs)
                ea2 = jnp.where(lane_lo, ea[0], ea[1])
                wcol2 = jnp.where(lane_lo, wcol[0], wcol[1])
                y_parts.append(y_diag + y_off[:, k * LANES:(k + 1) * LANES] * ea2 + xs * dskip_ref[:, cols])
                xw_parts.append((xs * wcol2).astype(BF16))
                decay_parts.append(ea2[Q - 1:Q, :])
                yield 250
            xw = jnp.concatenate(xw_parts, axis=1)
            new = lax.dot_general(bg, xw, (((0,), (0,)), ((), ())), preferred_element_type=F32)
            decay = jnp.concatenate(decay_parts, axis=1)
            state_ref[:, gcols] = state_ref[:, gcols] * decay + new
            yz = jnp.concatenate(y_parts, axis=1) * _silu(zs_ref[rows, gcols])
            yn_ref[rows, gcols] = _rms(yz, snw_ref[:, gcols]).astype(BF16)
            yield 250
    next_slot[0][:, 0:tail, :] = xraw_ref[:, tl:tl + tail, :]

    for c0 in range(0, D_MODEL, MXU_COLS):
        cs = slice(c0, c0 + MXU_COLS)
        ms_ref[:, cs] = (gss_ref[:, cs] * _dot(yn_ref[...], wbs_ref[:, cs])).astype(BF16)
        yield 400


def _mixer_in_kernel(x_ref, nw_ref, wmain_ref, wtail_ref,
                     convw_ref, convb_ref, dtb_ref, alog_ref, dskip_ref, snw_ref, wbs_ref,
                     u_ref, zs5_ref, g5_ref, mso_ref, mse_ref,
                     xraw0, dt0, zs0, gss0, xraw1, dt1, zs1, gss1, hb_ref, yn_ref, state_ref,
                     *, tiles_per_seq):
    m = pl.program_id(0)
    tl = dt0.shape[0]
    slots = ((xraw0, dt0, zs0, gss0), (xraw1, dt1, zs1, gss1))

    @pl.when(m == 0)
    def _():
        for ref in slots[0]:
            ref[...] = jnp.zeros(ref.shape, ref.dtype)
        state_ref[...] = jnp.zeros(state_ref.shape, F32)

    proj = functools.partial(_proj_items, x_ref, nw_ref, wmain_ref, wtail_ref, dtb_ref, u_ref, zs5_ref, g5_ref,
                             hb_ref)
    ssd = functools.partial(_ssd_items, convw_ref=convw_ref, convb_ref=convb_ref, alog_ref=alog_ref,
                            dskip_ref=dskip_ref, snw_ref=snw_ref, wbs_ref=wbs_ref,
                            yn_ref=yn_ref, state_ref=state_ref)

    _run_interleaved([ssd(slots[0], slots[1], ms_ref=mso_ref), proj(rows=slice(0, tl), slot=slots[1])])

    @pl.when(lax.rem(2 * m, tiles_per_seq) == 0)
    def _():
        xraw1[:, 0:SUBLANES, :] = jnp.zeros((xraw1.shape[0], SUBLANES, LANES), F32)
        state_ref[...] = jnp.zeros(state_ref.shape, F32)

    _run_interleaved([ssd(slots[1], slots[0], ms_ref=mse_ref), proj(rows=slice(tl, 2 * tl), slot=slots[0])])


def _const_spec(shape):
    zeros = (0,) * len(shape)
    return pl.BlockSpec(shape, lambda *_: zeros)


def _mixer_in(x, norm_w, w_in, conv_w, conv_b, dt_bias, a_log, d_skip, ssd_norm_w, w_br_ssd):
    bsz, seq, d = x.shape
    tl = TL_IN
    assert seq % (2 * tl) == 0 and TL_OUT % (2 * tl) == 0 and seq % TL_OUT == 0
    rows_total = bsz * seq
    nsteps = rows_total // (2 * tl)
    n_main = 2 * S5_WIDTH + SSD_WIDTH + SSD_CONV_DIM
    pad = LANES - SSD_HEADS
    w_main = w_in[:, :n_main].astype(BF16)
    w_tail = jnp.concatenate([w_in[:, n_main:n_main + SSD_HEADS], jnp.zeros((d, pad), w_in.dtype),
                              w_in[:, n_main + SSD_HEADS:]], axis=1).astype(BF16)
    dtb = jnp.pad(dt_bias.reshape(1, SSD_HEADS), ((0, 0), (0, pad)))
    alog = jnp.pad(a_log.reshape(1, SSD_HEADS), ((0, 0), (0, pad)))
    dskip = jnp.repeat(d_skip, SSD_HEADDIM).reshape(1, SSD_WIDTH)
    consts = [norm_w.reshape(1, d), w_main, w_tail, conv_w, conv_b.reshape(1, -1), dtb, alog,
              dskip, ssd_norm_w.reshape(1, -1), w_br_ssd.astype(BF16)]
    spb = seq // (2 * tl)
    proj_spec = lambda w: pl.BlockSpec((2 * tl, w), lambda m: (jnp.minimum(m, nsteps - 1), 0))

    def seq_spec(w):
        def index(m):
            mm = jnp.minimum(m, nsteps - 1)
            return mm // spb, mm % spb, 0
        return pl.BlockSpec((None, 2 * tl, w), index)
    slot = [pltpu.VMEM((SSD_CONV_DIM // LANES, tl + SUBLANES, LANES), F32), pltpu.VMEM((tl, LANES), F32),
            pltpu.VMEM((tl, SSD_WIDTH), F32), pltpu.VMEM((tl, D_MODEL), F32)]
    return pl.pallas_call(
        functools.partial(_mixer_in_kernel, tiles_per_seq=seq // tl),
        grid=(nsteps + 1,),
        in_specs=[seq_spec(d)] + [_const_spec(c.shape) for c in consts],
        out_specs=[seq_spec(S5_WIDTH), proj_spec(S5_WIDTH), proj_spec(D_MODEL),
                   pl.BlockSpec((tl, D_MODEL), lambda m: (jnp.maximum(m - 1, 0), 0)),
                   pl.BlockSpec((tl, D_MODEL), lambda m: (m, 0))],
        out_shape=[jax.ShapeDtypeStruct((bsz, seq, S5_WIDTH), F32),
                   jax.ShapeDtypeStruct((rows_total, S5_WIDTH), BF16),
                   jax.ShapeDtypeStruct((rows_total, D_MODEL), BF16),
                   jax.ShapeDtypeStruct((nsteps * tl, D_MODEL), BF16),
                   jax.ShapeDtypeStruct(((nsteps + 1) * tl, D_MODEL), BF16)],
        scratch_shapes=slot + slot + [pltpu.VMEM((tl, D_MODEL), BF16),
                                      pltpu.VMEM((tl, SSD_WIDTH), BF16),
                                      pltpu.VMEM((SSD_STATE, SSD_WIDTH), F32)],
        compiler_params=pltpu.CompilerParams(dimension_semantics=("arbitrary",),
                                             vmem_limit_bytes=VMEM_LIMIT_BYTES),
        name="mixer_in",
    )(x, *consts)


def _s5_scan_kernel(u_ref, bh_ref, ch_ref, lam_ref, d_ref, wglu_ref, bglu_ref, y_ref,
                    st_ref, utm_ref, ytm_ref):
    bsz, steps, _ = u_ref.shape

    @pl.when(pl.program_id(0) == 0)
    def _():
        st_ref[...] = jnp.zeros(st_ref.shape, F32)

    nlt = utm_ref.shape[0]
    for b in range(bsz):
        for c in range(nlt):
            utm_ref[c, pl.ds(b, steps, stride=bsz), :] = u_ref[b, :, c * LANES:(c + 1) * LANES]
    u = jnp.concatenate([utm_ref[c] for c in range(nlt)], axis=1)
    ub = u.astype(BF16)
    ys = []
    for q in range(S5_HALVES):
        ubq = ub[:, q * S5_HCH:(q + 1) * S5_HCH]
        yq = None
        for k in range(S5_HST // S5_CW):
            cols = slice(2 * k * S5_CW, 2 * (k + 1) * S5_CW)
            bu = _dot(ubq, bh_ref[q, :, cols])
            lam = lam_ref[q:q + 1, cols]
            lr = jnp.broadcast_to(lam[:, :S5_CW], (bsz, S5_CW))
            li = jnp.broadcast_to(lam[:, S5_CW:], (bsz, S5_CW))
            st = st_ref[q, :, cols]
            sr, si = st[:, :S5_CW], st[:, S5_CW:]
            hist = []
            for t in range(steps):
                r = slice(t * bsz, (t + 1) * bsz)
                sr, si = (lr * sr - li * si + bu[r, :S5_CW],
                          lr * si + li * sr + bu[r, S5_CW:])
                hist.append(jnp.concatenate([sr, si], axis=1).astype(BF16))
            st_ref[q, :, cols] = jnp.concatenate([sr, si], axis=1)
            part = _dot(jnp.concatenate(hist, axis=0), ch_ref[q, cols, :])
            yq = part if yq is None else yq + part
        ys.append(yq)
    y = jax.nn.gelu(jnp.concatenate(ys, axis=1) + d_ref[...] * u)
    y = y * _sigmoid(_dot(y.astype(BF16), wglu_ref[...]) + bglu_ref[...])
    for c in range(nlt):
        ytm_ref[c] = y[:, c * LANES:(c + 1) * LANES]
    for b in range(bsz):
        for c in range(nlt):
            y_ref[b, :, c * LANES:(c + 1) * LANES] = ytm_ref[c, pl.ds(b, steps, stride=bsz), :].astype(BF16)


def _s5_scan(u, bh, ch, lam, s5_d, w_glu, b_glu):
    bsz, seq, _ = u.shape
    steps = TT_S5
    rows = steps * bsz
    consts = [bh, ch, lam, s5_d.reshape(1, -1), w_glu.astype(BF16), b_glu.reshape(1, -1)]
    return pl.pallas_call(
        _s5_scan_kernel,
        grid=(seq // steps,),
        in_specs=[pl.BlockSpec((bsz, steps, S5_WIDTH), lambda i: (0, i, 0))]
                 + [_const_spec(c.shape) for c in consts],
        out_specs=pl.BlockSpec((bsz, steps, S5_WIDTH), lambda i: (0, i, 0)),
        out_shape=jax.ShapeDtypeStruct((bsz, seq, S5_WIDTH), BF16),
        scratch_shapes=[pltpu.VMEM((S5_HALVES, bsz, 2 * S5_HST), F32),
                        pltpu.VMEM((S5_WIDTH // LANES, rows, LANES), F32),
                        pltpu.VMEM((S5_WIDTH // LANES, rows, LANES), F32)],
        compiler_params=pltpu.CompilerParams(dimension_semantics=("arbitrary",),
                                             vmem_limit_bytes=VMEM_LIMIT_BYTES),
        name="s5_scan",
    )(u, *consts)


def _merge_rows(x_ref, p_ref, y5_ref, zs5_ref, g5_ref, ms_ref, wb5_ref, wout_ref, pnw_ref, wpg_ref, wpp_ref,
                fnw_ref, o_ref, rows, ms_rows):
    y5 = (y5_ref[rows, :].astype(F32) * zs5_ref[rows, :].astype(F32)).astype(BF16)
    yield 1
    branch = _dot(y5, wb5_ref[...])
    yield 1
    merged = (g5_ref[rows, :].astype(F32) * branch + ms_ref[ms_rows, :].astype(F32)).astype(BF16)
    yield 1
    h = x_ref[rows, :] + _dot(merged, wout_ref[...])
    yield 1
    hn = _rms(h, pnw_ref[...]).astype(BF16)
    yield 1
    gate = _sigmoid(_dot(hn, wpg_ref[...]))
    yield 1
    h = h + gate * _dot(p_ref[rows, :].astype(BF16), wpp_ref[...])
    yield 1
    o_ref[rows, :] = _rms(h, fnw_ref[...])
    yield 1


def _merge_out_kernel(x_ref, p_ref, y5_ref, zs5_ref, g5_ref, mse_ref, mso_ref, wb5_ref, wout_ref, pnw_ref,
                      wpg_ref, wpp_ref, fnw_ref, o_ref):
    sub = TL_IN
    _run_interleaved([
        _merge_rows(x_ref, p_ref, y5_ref, zs5_ref, g5_ref, (mse_ref, mso_ref)[r % 2], wb5_ref, wout_ref, pnw_ref,
                    wpg_ref, wpp_ref, fnw_ref, o_ref, slice(r * sub, (r + 1) * sub),
                    slice((r // 2) * sub, (r // 2 + 1) * sub))
        for r in range(x_ref.shape[0] // sub)])


def _merge_out(x, p, y5, zs5, g5, ms_even, ms_odd, w_br_s5, w_out, ple_norm_w, w_ple_gate, w_ple_proj,
               final_norm_w):
    bsz, seq, d = x.shape
    tl = TL_OUT
    consts = [w_br_s5.astype(BF16), w_out.astype(BF16), ple_norm_w.reshape(1, d), w_ple_gate.astype(BF16),
              w_ple_proj.astype(BF16), final_norm_w.reshape(1, d)]
    tok_spec = lambda w, rows=tl: pl.BlockSpec((rows, w), lambda b, t: (b * (seq // tl) + t, 0))
    return pl.pallas_call(
        _merge_out_kernel,
        grid=(bsz, seq // tl),
        in_specs=[pl.BlockSpec((None, tl, d), lambda b, t: (b, t, 0)),
                  pl.BlockSpec((None, tl, PLE_DIM), lambda b, t: (b, t, 0)),
                  pl.BlockSpec((None, tl, S5_WIDTH), lambda b, t: (b, t, 0)),
                  tok_spec(S5_WIDTH), tok_spec(D_MODEL),
                  tok_spec(D_MODEL, tl // 2), tok_spec(D_MODEL, tl // 2)]
                 + [_const_spec(c.shape) for c in consts],
        out_specs=pl.BlockSpec((None, tl, d), lambda b, t: (b, t, 0)),
        out_shape=jax.ShapeDtypeStruct((bsz, seq, d), F32),
        compiler_params=pltpu.CompilerParams(dimension_semantics=("parallel", "parallel"),
                                             vmem_limit_bytes=VMEM_LIMIT_BYTES),
        name="merge_out",
    )(x, p, y5, zs5, g5, ms_even, ms_odd, *consts)


def kernel(x, p, norm_w, w_in, s5_a_re, s5_a_im, s5_b_re, s5_b_im, s5_c_re, s5_c_im, s5_d, s5_log_step, s5_w_glu, s5_b_glu, ssd_conv_w, ssd_conv_b, ssd_dt_bias, ssd_a_log, ssd_d, ssd_norm_w, w_br_s5, w_br_ssd, w_out, ple_norm_w, w_ple_gate, w_ple_proj, final_norm_w):
    bsz, seq, _ = x.shape
    i = 0
    assert norm_w.shape[0] == 1

    lam, bh, ch = _s5_discretize(s5_a_re[i], s5_a_im[i], s5_log_step[i], s5_b_re[i], s5_b_im[i],
                                 s5_c_re[i], s5_c_im[i])

    u, zs5, g5, ms_odd, ms_even = _mixer_in(x, norm_w[i], w_in[i], ssd_conv_w[i], ssd_conv_b[i],
                                            ssd_dt_bias[i], ssd_a_log[i], ssd_d[i], ssd_norm_w[i], w_br_ssd[i])
    y5 = _s5_scan(u, bh, ch, lam, s5_d[i], s5_w_glu[i], s5_b_glu[i])
    return _merge_out(x, p[i], y5, zs5, g5, ms_even, ms_odd, w_br_s5[i], w_out[i], ple_norm_w[i],
                      w_ple_gate[i], w_ple_proj[i], final_norm_w)
```

```python
import functools

import jax
import jax.numpy as jnp
from jax import lax
from jax.experimental import pallas as pl
from jax.experimental.pallas import tpu as pltpu

F32 = jnp.float32
BF16 = jnp.bfloat16

D_MODEL = 1024
PLE_DIM = 256
RMS_EPS = 1e-6
LOG2_E = 1.4426950408889634
S5_WIDTH = 512
S5_GROUP = 16
S5_GROUPS = S5_WIDTH // S5_GROUP
S5_STATE = 64
S5_NSTATE = S5_GROUPS * S5_STATE
S5_HALVES = 2
S5_HCH = S5_WIDTH // S5_HALVES
S5_HST = S5_NSTATE // S5_HALVES
SSD_WIDTH = 1536
SSD_HEADDIM = 64
SSD_HEADS = SSD_WIDTH // SSD_HEADDIM
SSD_GROUPS = 4
SSD_HPG = SSD_HEADS // SSD_GROUPS
SSD_STATE = 128
SSD_CONV = 4
SSD_BC = SSD_GROUPS * SSD_STATE
SSD_CONV_DIM = SSD_WIDTH + 2 * SSD_BC
SSD_GW = SSD_WIDTH // SSD_GROUPS
N_BRANCH = 2

MXU_COLS = 256
LANES = 128
SUBLANES = 8
VMEM_LIMIT_BYTES = 56 * 1024 * 1024

Q = 128
TL_IN = 256
TT_S5 = 64
S5_PARTS = 2
S5_LAG = 2
S5_CW = 256
TL_OUT = 4 * TL_IN


def _sigmoid(v):
    return jax.nn.sigmoid(v)


def _silu(v):
    return v * jax.nn.sigmoid(v)


def _softplus(v):
    return jnp.maximum(v, 0.0) + jnp.log1p(jnp.exp(-jnp.abs(v)))


def _rms(v, w):
    return v * lax.rsqrt(jnp.mean(v * v, axis=-1, keepdims=True) + RMS_EPS) * w


def _dot(a, b):
    return jnp.dot(a, b, preferred_element_type=F32)


def _s5_pair_blocks(re, im):
    blocks = []
    for k in range(S5_HST // S5_CW):
        blocks += [v[:, k * S5_CW:(k + 1) * S5_CW] for v in (re, im)]
    return jnp.concatenate(blocks, axis=1)


def _s5_half_maps(re_t, im_t, q):
    half = slice(q * S5_HST, (q + 1) * S5_HST)
    reps = S5_HCH // S5_GROUP
    row = lax.broadcasted_iota(jnp.int32, (S5_HCH, S5_HST), 0)
    col = lax.broadcasted_iota(jnp.int32, (S5_HCH, S5_HST), 1)
    own = (row // S5_GROUP) == (col // S5_STATE)
    re, im = (jnp.where(own, jnp.concatenate([v[:, half]] * reps, axis=0), 0.0) for v in (re_t, im_t))
    return _s5_pair_blocks(re, im)


def _s5_disc_kernel(are_ref, aim_ref, lstep_ref, bre_ref, bim_ref, cre_ref, cim_ref,
                    lam_ref, bh_ref, ch_ref):
    a_re = are_ref[...]
    a_im = aim_ref[...]
    step = jnp.exp(lstep_ref[...])
    mag = jnp.exp(a_re * step)
    lb_re = mag * jnp.cos(a_im * step)
    lb_im = mag * jnp.sin(a_im * step)
    den = a_re * a_re + a_im * a_im
    n_re = lb_re - 1.0
    n_im = lb_im
    f_re = (n_re * a_re + n_im * a_im) / den
    f_im = (n_im * a_re - n_re * a_im) / den
    b_re = bre_ref[...]
    b_im = bim_ref[...]
    bb_re = f_re * b_re - f_im * b_im
    bb_im = f_re * b_im + f_im * b_re
    c_re = cre_ref[...]
    nc_im = -cim_ref[...]
    for q in range(S5_HALVES):
        bh_ref[q] = _s5_half_maps(bb_re, bb_im, q).astype(BF16)
        ch_ref[q] = _s5_half_maps(c_re, nc_im, q).T.astype(BF16)
        half = slice(q * S5_HST, (q + 1) * S5_HST)
        lam_ref[q:q + 1, :] = _s5_pair_blocks(lb_re[:, half], lb_im[:, half])


def _s5_discretize(a_re, a_im, log_step, b_re, b_im, c_re, c_im):
    n = S5_NSTATE
    row = lambda v: v.reshape(1, n)
    b_t = lambda v: jnp.transpose(v, (2, 0, 1)).reshape(S5_GROUP, n)
    c_t = lambda v: jnp.transpose(v, (1, 0, 2)).reshape(S5_GROUP, n)
    lstep = jnp.repeat(log_step, S5_STATE).reshape(1, n)
    return pl.pallas_call(
        _s5_disc_kernel,
        out_shape=[jax.ShapeDtypeStruct((S5_HALVES, 2 * S5_HST), F32),
                   jax.ShapeDtypeStruct((S5_HALVES, S5_HCH, 2 * S5_HST), BF16),
                   jax.ShapeDtypeStruct((S5_HALVES, 2 * S5_HST, S5_HCH), BF16)],
        name="s5_discretize",
    )(row(a_re), row(a_im), lstep, b_t(b_re), b_t(b_im), c_t(c_re), c_t(c_im))


def _run_interleaved(streams):
    acc = [0.0] * len(streams)
    live = list(range(len(streams)))
    while live:
        i = min(live, key=lambda j: acc[j])
        try:
            acc[i] += next(streams[i])
        except StopIteration:
            live.remove(i)


def _proj_items(x_ref, nw_ref, wmain_ref, wtail_ref, dtb_ref, u_ref, zs5_ref, g5_ref, hb_ref, rows, slot):
    xraw_ref, dt_ref, zs_ref, gss_ref = slot
    tl = hb_ref.shape[0]
    hb_ref[...] = _rms(x_ref[rows, :], nw_ref[...]).astype(BF16)
    yield 300

    def blocks(w_ref, start, n):
        for c0 in range(0, n, MXU_COLS):
            c1 = min(c0 + MXU_COLS, n)
            yield slice(c0, c1), _dot(hb_ref[...], w_ref[:, start + c0:start + c1])

    cost = tl
    z5_off = S5_WIDTH
    zs_off = 2 * S5_WIDTH
    xbc_off = zs_off + SSD_WIDTH
    for cs, r in blocks(wmain_ref, xbc_off, SSD_CONV_DIM):
        for j in range(0, cs.stop - cs.start, LANES):
            xraw_ref[(cs.start + j) // LANES, SUBLANES:SUBLANES + tl, :] = r[:, j:j + LANES]
        yield cost
    for cs, r in blocks(wtail_ref, 0, LANES):
        dt_ref[:, cs] = _softplus(r + dtb_ref[:, cs])
        yield cost
    for cs, r in blocks(wmain_ref, 0, S5_WIDTH):
        u_ref[rows, cs] = r
        yield cost
    for cs, r in blocks(wmain_ref, z5_off, S5_WIDTH):
        zs5_ref[rows, cs] = _silu(r).astype(BF16)
        yield cost
    for cs, r in blocks(wtail_ref, LANES, N_BRANCH * D_MODEL):
        g = _sigmoid(r)
        if cs.start < D_MODEL:
            g5_ref[rows, cs] = g.astype(BF16)
        else:
            gss_ref[:, cs.start - D_MODEL:cs.stop - D_MODEL] = g
        yield cost
    for cs, r in blocks(wmain_ref, zs_off, SSD_WIDTH):
        zs_ref[:, cs] = r
        yield cost


def _cumsum_rows(tril_b, v):
    hi = v.astype(BF16)
    r1 = v - hi.astype(F32)
    mid = r1.astype(BF16)
    lo = (r1 - mid.astype(F32)).astype(BF16)
    w = v.shape[1]
    s = _dot(tril_b, jnp.concatenate([hi, mid, lo], axis=1))
    return s[:, :w] + s[:, w:2 * w] + s[:, 2 * w:]


def _ssd_items(slot, next_slot, convw_ref, convb_ref, alog_ref, dskip_ref, snw_ref, wbs_ref,
               ms_ref, yn_ref, state_ref):
    xraw_ref, dt_ref, zs_ref, gss_ref = slot
    tl = dt_ref.shape[0]
    tail = SUBLANES

    def conv_tile(ct, r0):
        cs = slice(ct * LANES, (ct + 1) * LANES)
        acc = convb_ref[:, cs]
        for k in range(SSD_CONV):
            off = r0 + tail - (SSD_CONV - 1) + k
            acc = acc + convw_ref[k:k + 1, cs] * xraw_ref[ct, off:off + Q, :]
        return _silu(acc)

    a_row = -jnp.exp(alog_ref[...])
    ri = lax.broadcasted_iota(jnp.int32, (Q, Q), 0)
    ci = lax.broadcasted_iota(jnp.int32, (Q, Q), 1)
    causal = ci <= ri
    tril_b = causal.astype(BF16)
    lane_lo = lax.broadcasted_iota(jnp.int32, (Q, LANES), 1) < SSD_HEADDIM
    b_tile = SSD_WIDTH // LANES
    c_tile = (SSD_WIDTH + SSD_BC) // LANES

    for c in range(tl // Q):
        r0 = c * Q
        rows = slice(r0, r0 + Q)
        dt = dt_ref[rows, :]
        a_cum = _cumsum_rows(tril_b, dt * a_row) * LOG2_E
        a_cum_t = a_cum.T
        row_t = a_cum_t - jnp.log2(dt.T)
        ea_all = jnp.exp2(a_cum)
        wcol_all = jnp.exp2(a_cum[Q - 1:Q, :] - a_cum) * dt
        yield 100
        for g in range(SSD_GROUPS):
            bg = conv_tile(b_tile + g, r0).astype(BF16)
            cg = conv_tile(c_tile + g, r0).astype(BF16)
            scores = lax.dot_general(cg, bg, (((1,), (1,)), ((), ())), preferred_element_type=F32)
            gcols = slice(g * SSD_GW, (g + 1) * SSD_GW)
            y_off = _dot(cg, state_ref[:, gcols].astype(BF16))
            yield 150
            xw_parts = []
            decay_parts = []
            y_parts = []
            for k in range(SSD_HPG // 2):
                cols = slice(g * SSD_GW + k * LANES, g * SSD_GW + (k + 1) * LANES)
                xs = conv_tile(cols.start // LANES, r0)
                wts, ea, wcol = [], [], []
                for h in (g * SSD_HPG + 2 * k, g * SSD_HPG + 2 * k + 1):
                    seg = jnp.broadcast_to(a_cum[:, h:h + 1], (Q, Q)) - row_t[h:h + 1, :]
                    lmat = jnp.exp2(jnp.where(causal, seg, -jnp.inf))
                    wts.append((scores * lmat).astype(BF16))
                    ea.append(jnp.broadcast_to(ea_all[:, h:h + 1], (Q, LANES)))
                    wcol.append(jnp.broadcast_to(wcol_all[:, h:h + 1], (Q, LANES)))
                xb = xs.astype(BF16)
                zero = jnp.zeros_like(xb)
                rhs = jnp.concatenate([jnp.where(lane_lo, xb, zero), jnp.where(lane_lo, zero, xb)], axis=0)
                y_diag = _dot(jnp.concatenate(wts, axis=1), rhs)
                ea2 = jnp.where(lane_lo, ea[0], ea[1])
                wcol2 = jnp.where(lane_lo, wcol[0], wcol[1])
                y_parts.append(y_diag + y_off[:, k * LANES:(k + 1) * LANES] * ea2 + xs * dskip_ref[:, cols])
                xw_parts.append((xs * wcol2).astype(BF16))
                decay_parts.append(ea2[Q - 1:Q, :])
                yield 250
            xw = jnp.concatenate(xw_parts, axis=1)
            new = lax.dot_general(bg, xw, (((0,), (0,)), ((), ())), preferred_element_type=F32)
            decay = jnp.concatenate(decay_parts, axis=1)
            state_ref[:, gcols] = state_ref[:, gcols] * decay + new
            yz = jnp.concatenate(y_parts, axis=1) * _silu(zs_ref[rows, gcols])
            yn_ref[rows, gcols] = _rms(yz, snw_ref[:, gcols]).astype(BF16)
            yield 250
    next_slot[0][:, 0:tail, :] = xraw_ref[:, tl:tl + tail, :]

    for c0 in range(0, D_MODEL, MXU_COLS):
        cs = slice(c0, c0 + MXU_COLS)
        ms_ref[:, cs] = (gss_ref[:, cs] * _dot(yn_ref[...], wbs_ref[:, cs])).astype(BF16)
        yield 400


def _mixer_in_kernel(x_ref, nw_ref, wmain_ref, wtail_ref,
                     convw_ref, convb_ref, dtb_ref, alog_ref, dskip_ref, snw_ref, wbs_ref,
                     u_ref, zs5_ref, g5_ref, mso_ref, mse_ref,
                     xraw0, dt0, zs0, gss0, xraw1, dt1, zs1, gss1, hb_ref, yn_ref, state_ref,
                     *, tiles_per_seq):
    m = pl.program_id(0)
    tl = dt0.shape[0]
    slots = ((xraw0, dt0, zs0, gss0), (xraw1, dt1, zs1, gss1))

    @pl.when(m == 0)
    def _():
        for ref in slots[0]:
            ref[...] = jnp.zeros(ref.shape, ref.dtype)
        state_ref[...] = jnp.zeros(state_ref.shape, F32)

    proj = functools.partial(_proj_items, x_ref, nw_ref, wmain_ref, wtail_ref, dtb_ref, u_ref, zs5_ref, g5_ref,
                             hb_ref)
    ssd = functools.partial(_ssd_items, convw_ref=convw_ref, convb_ref=convb_ref, alog_ref=alog_ref,
                            dskip_ref=dskip_ref, snw_ref=snw_ref, wbs_ref=wbs_ref,
                            yn_ref=yn_ref, state_ref=state_ref)

    _run_interleaved([ssd(slots[0], slots[1], ms_ref=mso_ref), proj(rows=slice(0, tl), slot=slots[1])])

    @pl.when(lax.rem(2 * m, tiles_per_seq) == 0)
    def _():
        xraw1[:, 0:SUBLANES, :] = jnp.zeros((xraw1.shape[0], SUBLANES, LANES), F32)
        state_ref[...] = jnp.zeros(state_ref.shape, F32)

    _run_interleaved([ssd(slots[1], slots[0], ms_ref=mse_ref), proj(rows=slice(tl, 2 * tl), slot=slots[0])])


def _const_spec(shape):
    zeros = (0,) * len(shape)
    return pl.BlockSpec(shape, lambda *_: zeros)


def _mixer_in(x, norm_w, w_in, conv_w, conv_b, dt_bias, a_log, d_skip, ssd_norm_w, w_br_ssd):
    bsz, seq, d = x.shape
    tl = TL_IN
    assert seq % (2 * tl) == 0 and TL_OUT % (2 * tl) == 0 and seq % TL_OUT == 0
    rows_total = bsz * seq
    nsteps = rows_total // (2 * tl)
    n_main = 2 * S5_WIDTH + SSD_WIDTH + SSD_CONV_DIM
    pad = LANES - SSD_HEADS
    w_main = w_in.astype(BF16)
    w_tail = jnp.concatenate([w_main[:, n_main:n_main + SSD_HEADS], jnp.zeros((d, pad), BF16),
                              w_main[:, n_main + SSD_HEADS:]], axis=1)
    dtb = jnp.pad(dt_bias.reshape(1, SSD_HEADS), ((0, 0), (0, pad)))
    alog = jnp.pad(a_log.reshape(1, SSD_HEADS), ((0, 0), (0, pad)))
    dskip = jnp.repeat(d_skip, SSD_HEADDIM).reshape(1, SSD_WIDTH)
    consts = [norm_w.reshape(1, d), w_main, w_tail, conv_w, conv_b.reshape(1, -1), dtb, alog,
              dskip, ssd_norm_w.reshape(1, -1), w_br_ssd.astype(BF16)]
    spb = seq // (2 * tl)
    proj_spec = lambda w: pl.BlockSpec((2 * tl, w), lambda m: (jnp.minimum(m, nsteps - 1), 0))

    def seq_spec(w):
        def index(m):
            mm = jnp.minimum(m, nsteps - 1)
            return mm // spb, mm % spb, 0
        return pl.BlockSpec((None, 2 * tl, w), index)
    slot = [pltpu.VMEM((SSD_CONV_DIM // LANES, tl + SUBLANES, LANES), F32), pltpu.VMEM((tl, LANES), F32),
            pltpu.VMEM((tl, SSD_WIDTH), F32), pltpu.VMEM((tl, D_MODEL), F32)]
    return pl.pallas_call(
        functools.partial(_mixer_in_kernel, tiles_per_seq=seq // tl),
        grid=(nsteps + 1,),
        in_specs=[seq_spec(d)] + [_const_spec(c.shape) for c in consts],
        out_specs=[seq_spec(S5_WIDTH), proj_spec(S5_WIDTH), proj_spec(D_MODEL),
                   pl.BlockSpec((tl, D_MODEL), lambda m: (jnp.maximum(m - 1, 0), 0)),
                   pl.BlockSpec((tl, D_MODEL), lambda m: (m, 0))],
        out_shape=[jax.ShapeDtypeStruct((bsz, seq, S5_WIDTH), F32),
                   jax.ShapeDtypeStruct((rows_total, S5_WIDTH), BF16),
                   jax.ShapeDtypeStruct((rows_total, D_MODEL), BF16),
                   jax.ShapeDtypeStruct((nsteps * tl, D_MODEL), BF16),
                   jax.ShapeDtypeStruct(((nsteps + 1) * tl, D_MODEL), BF16)],
        scratch_shapes=slot + slot + [pltpu.VMEM((tl, D_MODEL), BF16),
                                      pltpu.VMEM((tl, SSD_WIDTH), BF16),
                                      pltpu.VMEM((SSD_STATE, SSD_WIDTH), F32)],
        compiler_params=pltpu.CompilerParams(dimension_semantics=("arbitrary",),
                                             vmem_limit_bytes=VMEM_LIMIT_BYTES),
        name="mixer_in",
    )(x, *consts)


def _s5_part_items(u_ref, bh_ref, ch_ref, lam_ref, d_ref, wglu_ref, bglu_ref, y_ref, st_ref, utm_ref, ytm_ref,
                   t0, steps):
    bsz = u_ref.shape[0]
    nlt = utm_ref.shape[0]
    r0 = t0 * bsz
    rows = slice(r0, r0 + steps * bsz)
    for b in range(bsz):
        for c in range(nlt):
            utm_ref[c, pl.ds(r0 + b, steps, stride=bsz), :] = u_ref[b, t0:t0 + steps, c * LANES:(c + 1) * LANES]
    yield 1
    u = jnp.concatenate([utm_ref[c, rows, :] for c in range(nlt)], axis=1)
    ub = u.astype(BF16)
    ys = []
    for q in range(S5_HALVES):
        ubq = ub[:, q * S5_HCH:(q + 1) * S5_HCH]
        yq = None
        for k in range(S5_HST // S5_CW):
            cols = slice(2 * k * S5_CW, 2 * (k + 1) * S5_CW)
            bu = _dot(ubq, bh_ref[q, :, cols])
            lam = lam_ref[q:q + 1, cols]
            lr = jnp.broadcast_to(lam[:, :S5_CW], (bsz, S5_CW))
            li = jnp.broadcast_to(lam[:, S5_CW:], (bsz, S5_CW))
            st = st_ref[q, :, cols]
            sr, si = st[:, :S5_CW], st[:, S5_CW:]
            hist = []
            for t in range(steps):
                r = slice(t * bsz, (t + 1) * bsz)
                sr, si = (lr * sr - li * si + bu[r, :S5_CW],
                          lr * si + li * sr + bu[r, S5_CW:])
                hist.append(jnp.concatenate([sr, si], axis=1).astype(BF16))
            st_ref[q, :, cols] = jnp.concatenate([sr, si], axis=1)
            part = _dot(jnp.concatenate(hist, axis=0), ch_ref[q, cols, :])
            yq = part if yq is None else yq + part
            yield 1
        ys.append(yq)
    y = jax.nn.gelu(jnp.concatenate(ys, axis=1) + d_ref[...] * u)
    yield 1
    y = y * _sigmoid(_dot(y.astype(BF16), wglu_ref[...]) + bglu_ref[...])
    for c in range(nlt):
        ytm_ref[c, rows, :] = y[:, c * LANES:(c + 1) * LANES]
    yield 1
    for b in range(bsz):
        for c in range(nlt):
            y_ref[b, t0:t0 + steps, c * LANES:(c + 1) * LANES] = (
                ytm_ref[c, pl.ds(r0 + b, steps, stride=bsz), :].astype(BF16))
    yield 1


def _delayed(stream, n):
    for _ in range(n):
        yield 1
    yield from stream


def _s5_scan_kernel(u_ref, bh_ref, ch_ref, lam_ref, d_ref, wglu_ref, bglu_ref, y_ref,
                    st_ref, utm_ref, ytm_ref):
    @pl.when(pl.program_id(0) == 0)
    def _():
        st_ref[...] = jnp.zeros(st_ref.shape, F32)

    part = u_ref.shape[1] // S5_PARTS
    _run_interleaved([
        _delayed(_s5_part_items(u_ref, bh_ref, ch_ref, lam_ref, d_ref, wglu_ref, bglu_ref, y_ref, st_ref,
                                utm_ref, ytm_ref, j * part, part), j * S5_LAG)
        for j in range(S5_PARTS)])


def _s5_scan(u, bh, ch, lam, s5_d, w_glu, b_glu):
    bsz, seq, _ = u.shape
    steps = TT_S5
    rows = steps * bsz
    consts = [bh, ch, lam, s5_d.reshape(1, -1), w_glu.astype(BF16), b_glu.reshape(1, -1)]
    return pl.pallas_call(
        _s5_scan_kernel,
        grid=(seq // steps,),
        in_specs=[pl.BlockSpec((bsz, steps, S5_WIDTH), lambda i: (0, i, 0))]
                 + [_const_spec(c.shape) for c in consts],
        out_specs=pl.BlockSpec((bsz, steps, S5_WIDTH), lambda i: (0, i, 0)),
        out_shape=jax.ShapeDtypeStruct((bsz, seq, S5_WIDTH), BF16),
        scratch_shapes=[pltpu.VMEM((S5_HALVES, bsz, 2 * S5_HST), F32),
                        pltpu.VMEM((S5_WIDTH // LANES, rows, LANES), F32),
                        pltpu.VMEM((S5_WIDTH // LANES, rows, LANES), F32)],
        compiler_params=pltpu.CompilerParams(dimension_semantics=("arbitrary",),
                                             vmem_limit_bytes=VMEM_LIMIT_BYTES),
        name="s5_scan",
    )(u, *consts)


def _merge_rows(x_ref, p_ref, y5_ref, zs5_ref, g5_ref, ms_ref, wb5_ref, wout_ref, pnw_ref, wpg_ref, wpp_ref,
                fnw_ref, o_ref, rows, ms_rows):
    y5 = (y5_ref[rows, :].astype(F32) * zs5_ref[rows, :].astype(F32)).astype(BF16)
    yield 1
    branch = _dot(y5, wb5_ref[...])
    yield 1
    merged = (g5_ref[rows, :].astype(F32) * branch + ms_ref[ms_rows, :].astype(F32)).astype(BF16)
    yield 1
    h = x_ref[rows, :] + _dot(merged, wout_ref[...])
    yield 1
    hn = _rms(h, pnw_ref[...]).astype(BF16)
    yield 1
    gate = _sigmoid(_dot(hn, wpg_ref[...]))
    yield 1
    h = h + gate * _dot(p_ref[rows, :].astype(BF16), wpp_ref[...])
    yield 1
    o_ref[rows, :] = _rms(h, fnw_ref[...])
    yield 1


def _merge_out_kernel(x_ref, p_ref, y5_ref, zs5_ref, g5_ref, mse_ref, mso_ref, wb5_ref, wout_ref, pnw_ref,
                      wpg_ref, wpp_ref, fnw_ref, o_ref):
    sub = TL_IN
    _run_interleaved([
        _merge_rows(x_ref, p_ref, y5_ref, zs5_ref, g5_ref, (mse_ref, mso_ref)[r % 2], wb5_ref, wout_ref, pnw_ref,
                    wpg_ref, wpp_ref, fnw_ref, o_ref, slice(r * sub, (r + 1) * sub),
                    slice((r // 2) * sub, (r // 2 + 1) * sub))
        for r in range(x_ref.shape[0] // sub)])


def _merge_out(x, p, y5, zs5, g5, ms_even, ms_odd, w_br_s5, w_out, ple_norm_w, w_ple_gate, w_ple_proj,
               final_norm_w):
    bsz, seq, d = x.shape
    tl = TL_OUT
    consts = [w_br_s5.astype(BF16), w_out.astype(BF16), ple_norm_w.reshape(1, d), w_ple_gate.astype(BF16),
              w_ple_proj.astype(BF16), final_norm_w.reshape(1, d)]
    tok_spec = lambda w, rows=tl: pl.BlockSpec((rows, w), lambda b, t: (b * (seq // tl) + t, 0))
    return pl.pallas_call(
        _merge_out_kernel,
        grid=(bsz, seq // tl),
        in_specs=[pl.BlockSpec((None, tl, d), lambda b, t: (b, t, 0)),
                  pl.BlockSpec((None, tl, PLE_DIM), lambda b, t: (b, t, 0)),
                  pl.BlockSpec((None, tl, S5_WIDTH), lambda b, t: (b, t, 0)),
                  tok_spec(S5_WIDTH), tok_spec(D_MODEL),
                  tok_spec(D_MODEL, tl // 2), tok_spec(D_MODEL, tl // 2)]
                 + [_const_spec(c.shape) for c in consts],
        out_specs=pl.BlockSpec((None, tl, d), lambda b, t: (b, t, 0)),
        out_shape=jax.ShapeDtypeStruct((bsz, seq, d), F32),
        compiler_params=pltpu.CompilerParams(dimension_semantics=("parallel", "parallel"),
                                             vmem_limit_bytes=VMEM_LIMIT_BYTES),
        name="merge_out",
    )(x, p, y5, zs5, g5, ms_even, ms_odd, *consts)


def kernel(x, p, norm_w, w_in, s5_a_re, s5_a_im, s5_b_re, s5_b_im, s5_c_re, s5_c_im, s5_d, s5_log_step, s5_w_glu, s5_b_glu, ssd_conv_w, ssd_conv_b, ssd_dt_bias, ssd_a_log, ssd_d, ssd_norm_w, w_br_s5, w_br_ssd, w_out, ple_norm_w, w_ple_gate, w_ple_proj, final_norm_w):
    bsz, seq, _ = x.shape
    i = 0
    assert norm_w.shape[0] == 1

    lam, bh, ch = _s5_discretize(s5_a_re[i], s5_a_im[i], s5_log_step[i], s5_b_re[i], s5_b_im[i],
                                 s5_c_re[i], s5_c_im[i])

    u, zs5, g5, ms_odd, ms_even = _mixer_in(x, norm_w[i], w_in[i], ssd_conv_w[i], ssd_conv_b[i],
                                            ssd_dt_bias[i], ssd_a_log[i], ssd_d[i], ssd_norm_w[i], w_br_ssd[i])
    y5 = _s5_scan(u, bh, ch, lam, s5_d[i], s5_w_glu[i], s5_b_glu[i])
    return _merge_out(x, p[i], y5, zs5, g5, ms_even, ms_odd, w_br_s5[i], w_out[i], ple_norm_w[i],
                      w_ple_gate[i], w_ple_proj[i], final_norm_w)
```

```python
import functools

import jax
import jax.numpy as jnp
from jax import lax
from jax.experimental import pallas as pl
from jax.experimental.pallas import tpu as pltpu

F32 = jnp.float32
BF16 = jnp.bfloat16

D_MODEL = 1024
PLE_DIM = 256
RMS_EPS = 1e-6
LOG2_E = 1.4426950408889634
S5_WIDTH = 512
S5_GROUP = 16
S5_GROUPS = S5_WIDTH // S5_GROUP
S5_STATE = 64
S5_NSTATE = S5_GROUPS * S5_STATE
S5_HALVES = 2
S5_HCH = S5_WIDTH // S5_HALVES
S5_HST = S5_NSTATE // S5_HALVES
SSD_WIDTH = 1536
SSD_HEADDIM = 64
SSD_HEADS = SSD_WIDTH // SSD_HEADDIM
SSD_GROUPS = 4
SSD_HPG = SSD_HEADS // SSD_GROUPS
SSD_STATE = 128
SSD_CONV = 4
SSD_BC = SSD_GROUPS * SSD_STATE
SSD_CONV_DIM = SSD_WIDTH + 2 * SSD_BC
SSD_GW = SSD_WIDTH // SSD_GROUPS
N_BRANCH = 2

MXU_COLS = 256
LANES = 128
SUBLANES = 8
VMEM_LIMIT_BYTES = 56 * 1024 * 1024

Q = 128
TL_IN = 256
TT_S5 = 64
S5_PARTS = 2
S5_LAG = 2
S5_CW = 128
MERGE_LAG = 1
TL_OUT = 4 * TL_IN


def _sigmoid(v):
    return jax.nn.sigmoid(v)


def _silu(v):
    return v * jax.nn.sigmoid(v)


def _softplus(v):
    return jnp.maximum(v, 0.0) + jnp.log1p(jnp.exp(-jnp.abs(v)))


def _rms(v, w):
    return v * lax.rsqrt(jnp.mean(v * v, axis=-1, keepdims=True) + RMS_EPS) * w


def _dot(a, b):
    return jnp.dot(a, b, preferred_element_type=F32)


def _s5_pair_blocks(re, im):
    blocks = []
    for k in range(S5_HST // S5_CW):
        blocks += [v[:, k * S5_CW:(k + 1) * S5_CW] for v in (re, im)]
    return jnp.concatenate(blocks, axis=1)


def _s5_half_maps(re_t, im_t, q):
    half = slice(q * S5_HST, (q + 1) * S5_HST)
    reps = S5_HCH // S5_GROUP
    row = lax.broadcasted_iota(jnp.int32, (S5_HCH, S5_HST), 0)
    col = lax.broadcasted_iota(jnp.int32, (S5_HCH, S5_HST), 1)
    own = (row // S5_GROUP) == (col // S5_STATE)
    re, im = (jnp.where(own, jnp.concatenate([v[:, half]] * reps, axis=0), 0.0) for v in (re_t, im_t))
    return _s5_pair_blocks(re, im)


def _s5_disc_kernel(are_ref, aim_ref, lstep_ref, bre_ref, bim_ref, cre_ref, cim_ref,
                    lam_ref, bh_ref, ch_ref):
    a_re = are_ref[...]
    a_im = aim_ref[...]
    step = jnp.exp(lstep_ref[...])
    mag = jnp.exp(a_re * step)
    lb_re = mag * jnp.cos(a_im * step)
    lb_im = mag * jnp.sin(a_im * step)
    den = a_re * a_re + a_im * a_im
    n_re = lb_re - 1.0
    n_im = lb_im
    f_re = (n_re * a_re + n_im * a_im) / den
    f_im = (n_im * a_re - n_re * a_im) / den
    b_re = bre_ref[...]
    b_im = bim_ref[...]
    bb_re = f_re * b_re - f_im * b_im
    bb_im = f_re * b_im + f_im * b_re
    c_re = cre_ref[...]
    nc_im = -cim_ref[...]
    for q in range(S5_HALVES):
        bh_ref[q] = _s5_half_maps(bb_re, bb_im, q).astype(BF16)
        ch_ref[q] = _s5_half_maps(c_re, nc_im, q).T.astype(BF16)
        half = slice(q * S5_HST, (q + 1) * S5_HST)
        lam_ref[q:q + 1, :] = _s5_pair_blocks(lb_re[:, half], lb_im[:, half])


def _s5_discretize(a_re, a_im, log_step, b_re, b_im, c_re, c_im):
    n = S5_NSTATE
    row = lambda v: v.reshape(1, n)
    b_t = lambda v: jnp.transpose(v, (2, 0, 1)).reshape(S5_GROUP, n)
    c_t = lambda v: jnp.transpose(v, (1, 0, 2)).reshape(S5_GROUP, n)
    lstep = jnp.repeat(log_step, S5_STATE).reshape(1, n)
    return pl.pallas_call(
        _s5_disc_kernel,
        out_shape=[jax.ShapeDtypeStruct((S5_HALVES, 2 * S5_HST), F32),
                   jax.ShapeDtypeStruct((S5_HALVES, S5_HCH, 2 * S5_HST), BF16),
                   jax.ShapeDtypeStruct((S5_HALVES, 2 * S5_HST, S5_HCH), BF16)],
        name="s5_discretize",
    )(row(a_re), row(a_im), lstep, b_t(b_re), b_t(b_im), c_t(c_re), c_t(c_im))


def _run_interleaved(streams):
    acc = [0.0] * len(streams)
    live = list(range(len(streams)))
    while live:
        i = min(live, key=lambda j: acc[j])
        try:
            acc[i] += next(streams[i])
        except StopIteration:
            live.remove(i)


def _proj_items(x_ref, nw_ref, wmain_ref, wtail_ref, dtb_ref, u_ref, zs5_ref, g5_ref, hb_ref, rows, slot):
    xraw_ref, dt_ref, zs_ref, gss_ref = slot
    tl = hb_ref.shape[0]
    hb_ref[...] = _rms(x_ref[rows, :], nw_ref[...]).astype(BF16)
    yield 300

    def blocks(w_ref, start, n):
        for c0 in range(0, n, MXU_COLS):
            c1 = min(c0 + MXU_COLS, n)
            yield slice(c0, c1), _dot(hb_ref[...], w_ref[:, start + c0:start + c1])

    cost = tl
    z5_off = S5_WIDTH
    zs_off = 2 * S5_WIDTH
    xbc_off = zs_off + SSD_WIDTH
    for cs, r in blocks(wmain_ref, xbc_off, SSD_CONV_DIM):
        for j in range(0, cs.stop - cs.start, LANES):
            xraw_ref[(cs.start + j) // LANES, SUBLANES:SUBLANES + tl, :] = r[:, j:j + LANES]
        yield cost
    for cs, r in blocks(wtail_ref, 0, LANES):
        dt_ref[:, cs] = _softplus(r + dtb_ref[:, cs])
        yield cost
    for cs, r in blocks(wmain_ref, 0, S5_WIDTH):
        u_ref[rows, cs] = r
        yield cost
    for cs, r in blocks(wmain_ref, z5_off, S5_WIDTH):
        zs5_ref[rows, cs] = _silu(r).astype(BF16)
        yield cost
    for cs, r in blocks(wtail_ref, LANES, N_BRANCH * D_MODEL):
        g = _sigmoid(r)
        if cs.start < D_MODEL:
            g5_ref[rows, cs] = g.astype(BF16)
        else:
            gss_ref[:, cs.start - D_MODEL:cs.stop - D_MODEL] = g
        yield cost
    for cs, r in blocks(wmain_ref, zs_off, SSD_WIDTH):
        zs_ref[:, cs] = r
        yield cost


def _cumsum_rows(tril_b, v):
    hi = v.astype(BF16)
    r1 = v - hi.astype(F32)
    mid = r1.astype(BF16)
    lo = (r1 - mid.astype(F32)).astype(BF16)
    w = v.shape[1]
    s = _dot(tril_b, jnp.concatenate([hi, mid, lo], axis=1))
    return s[:, :w] + s[:, w:2 * w] + s[:, 2 * w:]


def _ssd_items(slot, next_slot, convw_ref, convb_ref, alog_ref, dskip_ref, snw_ref, wbs_ref,
               ms_ref, yn_ref, state_ref):
    xraw_ref, dt_ref, zs_ref, gss_ref = slot
    tl = dt_ref.shape[0]
    tail = SUBLANES

    def conv_tile(ct, r0):
        cs = slice(ct * LANES, (ct + 1) * LANES)
        acc = convb_ref[:, cs]
        for k in range(SSD_CONV):
            off = r0 + tail - (SSD_CONV - 1) + k
            acc = acc + convw_ref[k:k + 1, cs] * xraw_ref[ct, off:off + Q, :]
        return _silu(acc)

    a_row = -jnp.exp(alog_ref[...])
    ri = lax.broadcasted_iota(jnp.int32, (Q, Q), 0)
    ci = lax.broadcasted_iota(jnp.int32, (Q, Q), 1)
    causal = ci <= ri
    tril_b = causal.astype(BF16)
    lane_lo = lax.broadcasted_iota(jnp.int32, (Q, LANES), 1) < SSD_HEADDIM
    b_tile = SSD_WIDTH // LANES
    c_tile = (SSD_WIDTH + SSD_BC) // LANES

    for c in range(tl // Q):
        r0 = c * Q
        rows = slice(r0, r0 + Q)
        dt = dt_ref[rows, :]
        a_cum = _cumsum_rows(tril_b, dt * a_row) * LOG2_E
        a_cum_t = a_cum.T
        row_t = a_cum_t - jnp.log2(dt.T)
        ea_all = jnp.exp2(a_cum)
        wcol_all = jnp.exp2(a_cum[Q - 1:Q, :] - a_cum) * dt
        yield 100
        for g in range(SSD_GROUPS):
            bg = conv_tile(b_tile + g, r0).astype(BF16)
            cg = conv_tile(c_tile + g, r0).astype(BF16)
            scores = lax.dot_general(cg, bg, (((1,), (1,)), ((), ())), preferred_element_type=F32)
            gcols = slice(g * SSD_GW, (g + 1) * SSD_GW)
            y_off = _dot(cg, state_ref[:, gcols].astype(BF16))
            yield 150
            xw_parts = []
            decay_parts = []
            y_parts = []
            for k in range(SSD_HPG // 2):
                cols = slice(g * SSD_GW + k * LANES, g * SSD_GW + (k + 1) * LANES)
                xs = conv_tile(cols.start // LANES, r0)
                wts, ea, wcol = [], [], []
                for h in (g * SSD_HPG + 2 * k, g * SSD_HPG + 2 * k + 1):
                    seg = jnp.broadcast_to(a_cum[:, h:h + 1], (Q, Q)) - row_t[h:h + 1, :]
                    lmat = jnp.exp2(jnp.where(causal, seg, -jnp.inf))
                    wts.append((scores * lmat).astype(BF16))
                    ea.append(jnp.broadcast_to(ea_all[:, h:h + 1], (Q, LANES)))
                    wcol.append(jnp.broadcast_to(wcol_all[:, h:h + 1], (Q, LANES)))
                xb = xs.astype(BF16)
                zero = jnp.zeros_like(xb)
                rhs = jnp.concatenate([jnp.where(lane_lo, xb, zero), jnp.where(lane_lo, zero, xb)], axis=0)
                y_diag = _dot(jnp.concatenate(wts, axis=1), rhs)
                ea2 = jnp.where(lane_lo, ea[0], ea[1])
                wcol2 = jnp.where(lane_lo, wcol[0], wcol[1])
                y_parts.append(y_diag + y_off[:, k * LANES:(k + 1) * LANES] * ea2 + xs * dskip_ref[:, cols])
                xw_parts.append((xs * wcol2).astype(BF16))
                decay_parts.append(ea2[Q - 1:Q, :])
                yield 250
            xw = jnp.concatenate(xw_parts, axis=1)
            new = lax.dot_general(bg, xw, (((0,), (0,)), ((), ())), preferred_element_type=F32)
            decay = jnp.concatenate(decay_parts, axis=1)
            state_ref[:, gcols] = state_ref[:, gcols] * decay + new
            yz = jnp.concatenate(y_parts, axis=1) * _silu(zs_ref[rows, gcols])
            yn_ref[rows, gcols] = _rms(yz, snw_ref[:, gcols]).astype(BF16)
            yield 250
    next_slot[0][:, 0:tail, :] = xraw_ref[:, tl:tl + tail, :]

    for c0 in range(0, D_MODEL, MXU_COLS):
        cs = slice(c0, c0 + MXU_COLS)
        ms_ref[:, cs] = (gss_ref[:, cs] * _dot(yn_ref[...], wbs_ref[:, cs])).astype(BF16)
        yield 400


def _mixer_in_kernel(x_ref, nw_ref, wmain_ref, wtail_ref,
                     convw_ref, convb_ref, dtb_ref, alog_ref, dskip_ref, snw_ref, wbs_ref,
                     u_ref, zs5_ref, g5_ref, mso_ref, mse_ref,
                     xraw0, dt0, zs0, gss0, xraw1, dt1, zs1, gss1, hb_ref, yn_ref, state_ref,
                     *, tiles_per_seq):
    m = pl.program_id(0)
    tl = dt0.shape[0]
    slots = ((xraw0, dt0, zs0, gss0), (xraw1, dt1, zs1, gss1))

    @pl.when(m == 0)
    def _():
        for ref in slots[0]:
            ref[...] = jnp.zeros(ref.shape, ref.dtype)
        state_ref[...] = jnp.zeros(state_ref.shape, F32)

    proj = functools.partial(_proj_items, x_ref, nw_ref, wmain_ref, wtail_ref, dtb_ref, u_ref, zs5_ref, g5_ref,
                             hb_ref)
    ssd = functools.partial(_ssd_items, convw_ref=convw_ref, convb_ref=convb_ref, alog_ref=alog_ref,
                            dskip_ref=dskip_ref, snw_ref=snw_ref, wbs_ref=wbs_ref,
                            yn_ref=yn_ref, state_ref=state_ref)

    _run_interleaved([ssd(slots[0], slots[1], ms_ref=mso_ref), proj(rows=slice(0, tl), slot=slots[1])])

    @pl.when(lax.rem(2 * m, tiles_per_seq) == 0)
    def _():
        xraw1[:, 0:SUBLANES, :] = jnp.zeros((xraw1.shape[0], SUBLANES, LANES), F32)
        state_ref[...] = jnp.zeros(state_ref.shape, F32)

    _run_interleaved([ssd(slots[1], slots[0], ms_ref=mse_ref), proj(rows=slice(tl, 2 * tl), slot=slots[0])])


def _const_spec(shape):
    zeros = (0,) * len(shape)
    return pl.BlockSpec(shape, lambda *_: zeros)


def _mixer_in(x, norm_w, w_in, conv_w, conv_b, dt_bias, a_log, d_skip, ssd_norm_w, w_br_ssd):
    bsz, seq, d = x.shape
    tl = TL_IN
    assert seq % (2 * tl) == 0 and TL_OUT % (2 * tl) == 0 and seq % TL_OUT == 0
    rows_total = bsz * seq
    nsteps = rows_total // (2 * tl)
    n_main = 2 * S5_WIDTH + SSD_WIDTH + SSD_CONV_DIM
    pad = LANES - SSD_HEADS
    w_main = w_in.astype(BF16)
    w_tail = jnp.concatenate([w_main[:, n_main:n_main + SSD_HEADS], jnp.zeros((d, pad), BF16),
                              w_main[:, n_main + SSD_HEADS:]], axis=1)
    dtb = jnp.pad(dt_bias.reshape(1, SSD_HEADS), ((0, 0), (0, pad)))
    alog = jnp.pad(a_log.reshape(1, SSD_HEADS), ((0, 0), (0, pad)))
    dskip = jnp.repeat(d_skip, SSD_HEADDIM).reshape(1, SSD_WIDTH)
    consts = [norm_w.reshape(1, d), w_main, w_tail, conv_w, conv_b.reshape(1, -1), dtb, alog,
              dskip, ssd_norm_w.reshape(1, -1), w_br_ssd.astype(BF16)]
    spb = seq // (2 * tl)
    proj_spec = lambda w: pl.BlockSpec((2 * tl, w), lambda m: (jnp.minimum(m, nsteps - 1), 0))

    def seq_spec(w):
        def index(m):
            mm = jnp.minimum(m, nsteps - 1)
            return mm // spb, mm % spb, 0
        return pl.BlockSpec((None, 2 * tl, w), index)
    slot = [pltpu.VMEM((SSD_CONV_DIM // LANES, tl + SUBLANES, LANES), F32), pltpu.VMEM((tl, LANES), F32),
            pltpu.VMEM((tl, SSD_WIDTH), F32), pltpu.VMEM((tl, D_MODEL), F32)]
    return pl.pallas_call(
        functools.partial(_mixer_in_kernel, tiles_per_seq=seq // tl),
        grid=(nsteps + 1,),
        in_specs=[seq_spec(d)] + [_const_spec(c.shape) for c in consts],
        out_specs=[seq_spec(S5_WIDTH), proj_spec(S5_WIDTH), proj_spec(D_MODEL),
                   pl.BlockSpec((tl, D_MODEL), lambda m: (jnp.maximum(m - 1, 0), 0)),
                   pl.BlockSpec((tl, D_MODEL), lambda m: (m, 0))],
        out_shape=[jax.ShapeDtypeStruct((bsz, seq, S5_WIDTH), F32),
                   jax.ShapeDtypeStruct((rows_total, S5_WIDTH), BF16),
                   jax.ShapeDtypeStruct((rows_total, D_MODEL), BF16),
                   jax.ShapeDtypeStruct((nsteps * tl, D_MODEL), BF16),
                   jax.ShapeDtypeStruct(((nsteps + 1) * tl, D_MODEL), BF16)],
        scratch_shapes=slot + slot + [pltpu.VMEM((tl, D_MODEL), BF16),
                                      pltpu.VMEM((tl, SSD_WIDTH), BF16),
                                      pltpu.VMEM((SSD_STATE, SSD_WIDTH), F32)],
        compiler_params=pltpu.CompilerParams(dimension_semantics=("arbitrary",),
                                             vmem_limit_bytes=VMEM_LIMIT_BYTES),
        name="mixer_in",
    )(x, *consts)


def _s5_part_items(u_ref, bh_ref, ch_ref, lam_ref, d_ref, wglu_ref, bglu_ref, y_ref, st_ref, utm_ref, ytm_ref,
                   t0, steps):
    bsz = u_ref.shape[0]
    nlt = utm_ref.shape[0]
    r0 = t0 * bsz
    rows = slice(r0, r0 + steps * bsz)
    for b in range(bsz):
        for c in range(nlt):
            utm_ref[c, pl.ds(r0 + b, steps, stride=bsz), :] = u_ref[b, t0:t0 + steps, c * LANES:(c + 1) * LANES]
    yield 1
    u = jnp.concatenate([utm_ref[c, rows, :] for c in range(nlt)], axis=1)
    ub = u.astype(BF16)
    ys = []
    for q in range(S5_HALVES):
        ubq = ub[:, q * S5_HCH:(q + 1) * S5_HCH]
        yq = None
        for k in range(S5_HST // S5_CW):
            cols = slice(2 * k * S5_CW, 2 * (k + 1) * S5_CW)
            bu = _dot(ubq, bh_ref[q, :, cols])
            lam = lam_ref[q:q + 1, cols]
            lr = jnp.broadcast_to(lam[:, :S5_CW], (bsz, S5_CW))
            li = jnp.broadcast_to(lam[:, S5_CW:], (bsz, S5_CW))
            st = st_ref[q, :, cols]
            sr, si = st[:, :S5_CW], st[:, S5_CW:]
            hist = []
            for t in range(steps):
                r = slice(t * bsz, (t + 1) * bsz)
                sr, si = (lr * sr - li * si + bu[r, :S5_CW],
                          lr * si + li * sr + bu[r, S5_CW:])
                hist.append(jnp.concatenate([sr, si], axis=1).astype(BF16))
            st_ref[q, :, cols] = jnp.concatenate([sr, si], axis=1)
            part = _dot(jnp.concatenate(hist, axis=0), ch_ref[q, cols, :])
            yq = part if yq is None else yq + part
            yield 1
        ys.append(yq)
    y = jax.nn.gelu(jnp.concatenate(ys, axis=1) + d_ref[...] * u)
    yield 1
    y = y * _sigmoid(_dot(y.astype(BF16), wglu_ref[...]) + bglu_ref[...])
    for c in range(nlt):
        ytm_ref[c, rows, :] = y[:, c * LANES:(c + 1) * LANES]
    yield 1
    for b in range(bsz):
        for c in range(nlt):
            y_ref[b, t0:t0 + steps, c * LANES:(c + 1) * LANES] = (
                ytm_ref[c, pl.ds(r0 + b, steps, stride=bsz), :].astype(BF16))
    yield 1


def _delayed(stream, n):
    for _ in range(n):
        yield 1
    yield from stream


def _s5_scan_kernel(u_ref, bh_ref, ch_ref, lam_ref, d_ref, wglu_ref, bglu_ref, y_ref,
                    st_ref, utm_ref, ytm_ref):
    @pl.when(pl.program_id(0) == 0)
    def _():
        st_ref[...] = jnp.zeros(st_ref.shape, F32)

    part = u_ref.shape[1] // S5_PARTS
    _run_interleaved([
        _delayed(_s5_part_items(u_ref, bh_ref, ch_ref, lam_ref, d_ref, wglu_ref, bglu_ref, y_ref, st_ref,
                                utm_ref, ytm_ref, j * part, part), j * S5_LAG)
        for j in range(S5_PARTS)])


def _s5_scan(u, bh, ch, lam, s5_d, w_glu, b_glu):
    bsz, seq, _ = u.shape
    steps = TT_S5
    rows = steps * bsz
    consts = [bh, ch, lam, s5_d.reshape(1, -1), w_glu.astype(BF16), b_glu.reshape(1, -1)]
    return pl.pallas_call(
        _s5_scan_kernel,
        grid=(seq // steps,),
        in_specs=[pl.BlockSpec((bsz, steps, S5_WIDTH), lambda i: (0, i, 0))]
                 + [_const_spec(c.shape) for c in consts],
        out_specs=pl.BlockSpec((bsz, steps, S5_WIDTH), lambda i: (0, i, 0)),
        out_shape=jax.ShapeDtypeStruct((bsz, seq, S5_WIDTH), BF16),
        scratch_shapes=[pltpu.VMEM((S5_HALVES, bsz, 2 * S5_HST), F32),
                        pltpu.VMEM((S5_WIDTH // LANES, rows, LANES), F32),
                        pltpu.VMEM((S5_WIDTH // LANES, rows, LANES), F32)],
        compiler_params=pltpu.CompilerParams(dimension_semantics=("arbitrary",),
                                             vmem_limit_bytes=VMEM_LIMIT_BYTES),
        name="s5_scan",
    )(u, *consts)


def _merge_rows(x_ref, p_ref, y5_ref, zs5_ref, g5_ref, ms_ref, wb5_ref, wout_ref, pnw_ref, wpg_ref, wpp_ref,
                fnw_ref, o_ref, rows, ms_rows):
    y5 = (y5_ref[rows, :].astype(F32) * zs5_ref[rows, :].astype(F32)).astype(BF16)
    yield 1
    branch = _dot(y5, wb5_ref[...])
    yield 1
    merged = (g5_ref[rows, :].astype(F32) * branch + ms_ref[ms_rows, :].astype(F32)).astype(BF16)
    yield 1
    h = x_ref[rows, :] + _dot(merged, wout_ref[...])
    yield 1
    hn = _rms(h, pnw_ref[...]).astype(BF16)
    yield 1
    gate = _sigmoid(_dot(hn, wpg_ref[...]))
    yield 1
    h = h + gate * _dot(p_ref[rows, :].astype(BF16), wpp_ref[...])
    yield 1
    o_ref[rows, :] = _rms(h, fnw_ref[...])
    yield 1


def _merge_out_kernel(x_ref, p_ref, y5_ref, zs5_ref, g5_ref, mse_ref, mso_ref, wb5_ref, wout_ref, pnw_ref,
                      wpg_ref, wpp_ref, fnw_ref, o_ref):
    sub = TL_IN
    _run_interleaved([
        _delayed(_merge_rows(x_ref, p_ref, y5_ref, zs5_ref, g5_ref, (mse_ref, mso_ref)[r % 2], wb5_ref, wout_ref,
                             pnw_ref, wpg_ref, wpp_ref, fnw_ref, o_ref, slice(r * sub, (r + 1) * sub),
                             slice((r // 2) * sub, (r // 2 + 1) * sub)), r * MERGE_LAG)
        for r in range(x_ref.shape[0] // sub)])


def _merge_out(x, p, y5, zs5, g5, ms_even, ms_odd, w_br_s5, w_out, ple_norm_w, w_ple_gate, w_ple_proj,
               final_norm_w):
    bsz, seq, d = x.shape
    tl = TL_OUT
    consts = [w_br_s5.astype(BF16), w_out.astype(BF16), ple_norm_w.reshape(1, d), w_ple_gate.astype(BF16),
              w_ple_proj.astype(BF16), final_norm_w.reshape(1, d)]
    tok_spec = lambda w, rows=tl: pl.BlockSpec((rows, w), lambda b, t: (b * (seq // tl) + t, 0))
    return pl.pallas_call(
        _merge_out_kernel,
        grid=(bsz, seq // tl),
        in_specs=[pl.BlockSpec((None, tl, d), lambda b, t: (b, t, 0)),
                  pl.BlockSpec((None, tl, PLE_DIM), lambda b, t: (b, t, 0)),
                  pl.BlockSpec((None, tl, S5_WIDTH), lambda b, t: (b, t, 0)),
                  tok_spec(S5_WIDTH), tok_spec(D_MODEL),
                  tok_spec(D_MODEL, tl // 2), tok_spec(D_MODEL, tl // 2)]
                 + [_const_spec(c.shape) for c in consts],
        out_specs=pl.BlockSpec((None, tl, d), lambda b, t: (b, t, 0)),
        out_shape=jax.ShapeDtypeStruct((bsz, seq, d), F32),
        compiler_params=pltpu.CompilerParams(dimension_semantics=("parallel", "parallel"),
                                             vmem_limit_bytes=VMEM_LIMIT_BYTES),
        name="merge_out",
    )(x, p, y5, zs5, g5, ms_even, ms_odd, *consts)


def kernel(x, p, norm_w, w_in, s5_a_re, s5_a_im, s5_b_re, s5_b_im, s5_c_re, s5_c_im, s5_d, s5_log_step, s5_w_glu, s5_b_glu, ssd_conv_w, ssd_conv_b, ssd_dt_bias, ssd_a_log, ssd_d, ssd_norm_w, w_br_s5, w_br_ssd, w_out, ple_norm_w, w_ple_gate, w_ple_proj, final_norm_w):
    bsz, seq, _ = x.shape
    i = 0
    assert norm_w.shape[0] == 1

    lam, bh, ch = _s5_discretize(s5_a_re[i], s5_a_im[i], s5_log_step[i], s5_b_re[i], s5_b_im[i],
                                 s5_c_re[i], s5_c_im[i])

    u, zs5, g5, ms_odd, ms_even = _mixer_in(x, norm_w[i], w_in[i], ssd_conv_w[i], ssd_conv_b[i],
                                            ssd_dt_bias[i], ssd_a_log[i], ssd_d[i], ssd_norm_w[i], w_br_ssd[i])
    y5 = _s5_scan(u, bh, ch, lam, s5_d[i], s5_w_glu[i], s5_b_glu[i])
    return _merge_out(x, p[i], y5, zs5, g5, ms_even, ms_odd, w_br_s5[i], w_out[i], ple_norm_w[i],
                      w_ple_gate[i], w_ple_proj[i], final_norm_w)
```

```python
import functools

import jax
import jax.numpy as jnp
from jax import lax
from jax.experimental import pallas as pl
from jax.experimental.pallas import tpu as pltpu

F32 = jnp.float32
BF16 = jnp.bfloat16

D_MODEL = 1024
PLE_DIM = 256
RMS_EPS = 1e-6
LOG2_E = 1.4426950408889634
S5_WIDTH = 512
S5_GROUP = 16
S5_GROUPS = S5_WIDTH // S5_GROUP
S5_STATE = 64
S5_NSTATE = S5_GROUPS * S5_STATE
S5_HALVES = 2
S5_HCH = S5_WIDTH // S5_HALVES
S5_HST = S5_NSTATE // S5_HALVES
SSD_WIDTH = 1536
SSD_HEADDIM = 64
SSD_HEADS = SSD_WIDTH // SSD_HEADDIM
SSD_GROUPS = 4
SSD_HPG = SSD_HEADS // SSD_GROUPS
SSD_STATE = 128
SSD_CONV = 4
SSD_BC = SSD_GROUPS * SSD_STATE
SSD_CONV_DIM = SSD_WIDTH + 2 * SSD_BC
SSD_GW = SSD_WIDTH // SSD_GROUPS
N_BRANCH = 2

MXU_COLS = 256
LANES = 128
SUBLANES = 8
VMEM_LIMIT_BYTES = 56 * 1024 * 1024

Q = 128
TL_IN = 256
TT_S5 = 64
S5_PARTS = 2
S5_LAG = 2
S5_CW = 128
MERGE_LAG = 1
TL_OUT = 4 * TL_IN


def _sigmoid(v):
    return jax.nn.sigmoid(v)


def _silu(v):
    return v * jax.nn.sigmoid(v)


def _softplus(v):
    return jnp.maximum(v, 0.0) + jnp.log1p(jnp.exp(-jnp.abs(v)))


def _rms(v, w):
    return v * lax.rsqrt(jnp.mean(v * v, axis=-1, keepdims=True) + RMS_EPS) * w


def _dot(a, b):
    return jnp.dot(a, b, preferred_element_type=F32)


def _s5_pair_blocks(re, im):
    blocks = []
    for k in range(S5_HST // S5_CW):
        blocks += [v[:, k * S5_CW:(k + 1) * S5_CW] for v in (re, im)]
    return jnp.concatenate(blocks, axis=1)


def _s5_half_maps(re_t, im_t, q):
    half = slice(q * S5_HST, (q + 1) * S5_HST)
    reps = S5_HCH // S5_GROUP
    row = lax.broadcasted_iota(jnp.int32, (S5_HCH, S5_HST), 0)
    col = lax.broadcasted_iota(jnp.int32, (S5_HCH, S5_HST), 1)
    own = (row // S5_GROUP) == (col // S5_STATE)
    re, im = (jnp.where(own, jnp.concatenate([v[:, half]] * reps, axis=0), 0.0) for v in (re_t, im_t))
    return _s5_pair_blocks(re, im)


def _s5_disc_kernel(are_ref, aim_ref, lstep_ref, bre_ref, bim_ref, cre_ref, cim_ref,
                    lam_ref, bh_ref, ch_ref):
    a_re = are_ref[...]
    a_im = aim_ref[...]
    step = jnp.exp(lstep_ref[...])
    mag = jnp.exp(a_re * step)
    lb_re = mag * jnp.cos(a_im * step)
    lb_im = mag * jnp.sin(a_im * step)
    den = a_re * a_re + a_im * a_im
    n_re = lb_re - 1.0
    n_im = lb_im
    f_re = (n_re * a_re + n_im * a_im) / den
    f_im = (n_im * a_re - n_re * a_im) / den
    b_re = bre_ref[...]
    b_im = bim_ref[...]
    bb_re = f_re * b_re - f_im * b_im
    bb_im = f_re * b_im + f_im * b_re
    c_re = cre_ref[...]
    nc_im = -cim_ref[...]
    for q in range(S5_HALVES):
        bh_ref[q] = _s5_half_maps(bb_re, bb_im, q).astype(BF16)
        ch_ref[q] = _s5_half_maps(c_re, nc_im, q).T.astype(BF16)
        half = slice(q * S5_HST, (q + 1) * S5_HST)
        lam_ref[q:q + 1, :] = _s5_pair_blocks(lb_re[:, half], lb_im[:, half])


def _s5_discretize(a_re, a_im, log_step, b_re, b_im, c_re, c_im):
    n = S5_NSTATE
    row = lambda v: v.reshape(1, n)
    b_t = lambda v: jnp.transpose(v, (2, 0, 1)).reshape(S5_GROUP, n)
    c_t = lambda v: jnp.transpose(v, (1, 0, 2)).reshape(S5_GROUP, n)
    lstep = jnp.repeat(log_step, S5_STATE).reshape(1, n)
    return pl.pallas_call(
        _s5_disc_kernel,
        out_shape=[jax.ShapeDtypeStruct((S5_HALVES, 2 * S5_HST), F32),
                   jax.ShapeDtypeStruct((S5_HALVES, S5_HCH, 2 * S5_HST), BF16),
                   jax.ShapeDtypeStruct((S5_HALVES, 2 * S5_HST, S5_HCH), BF16)],
        name="s5_discretize",
    )(row(a_re), row(a_im), lstep, b_t(b_re), b_t(b_im), c_t(c_re), c_t(c_im))


def _run_interleaved(streams):
    acc = [0.0] * len(streams)
    live = list(range(len(streams)))
    while live:
        i = min(live, key=lambda j: acc[j])
        try:
            acc[i] += next(streams[i])
        except StopIteration:
            live.remove(i)


def _proj_items(x_ref, nw_ref, wmain_ref, wtail_ref, dtb_ref, u_ref, zs5_ref, g5_ref, hb_ref, rows, slot):
    xraw_ref, dt_ref, zs_ref, gss_ref = slot
    tl = hb_ref.shape[0]
    hb_ref[...] = _rms(x_ref[rows, :], nw_ref[...]).astype(BF16)
    yield 300

    def blocks(w_ref, start, n):
        for c0 in range(0, n, MXU_COLS):
            c1 = min(c0 + MXU_COLS, n)
            yield slice(c0, c1), _dot(hb_ref[...], w_ref[:, start + c0:start + c1])

    cost = tl
    z5_off = S5_WIDTH
    zs_off = 2 * S5_WIDTH
    xbc_off = zs_off + SSD_WIDTH
    for cs, r in blocks(wmain_ref, xbc_off, SSD_CONV_DIM):
        for j in range(0, cs.stop - cs.start, LANES):
            xraw_ref[(cs.start + j) // LANES, SUBLANES:SUBLANES + tl, :] = r[:, j:j + LANES]
        yield cost
    for cs, r in blocks(wtail_ref, 0, LANES):
        dt_ref[:, cs] = _softplus(r + dtb_ref[:, cs])
        yield cost
    for cs, r in blocks(wmain_ref, 0, S5_WIDTH):
        u_ref[rows, cs] = r
        yield cost
    for cs, r in blocks(wmain_ref, z5_off, S5_WIDTH):
        zs5_ref[rows, cs] = _silu(r).astype(BF16)
        yield cost
    for cs, r in blocks(wtail_ref, LANES, N_BRANCH * D_MODEL):
        g = _sigmoid(r)
        if cs.start < D_MODEL:
            g5_ref[rows, cs] = g.astype(BF16)
        else:
            gss_ref[:, cs.start - D_MODEL:cs.stop - D_MODEL] = g
        yield cost
    for cs, r in blocks(wmain_ref, zs_off, SSD_WIDTH):
        zs_ref[:, cs] = r
        yield cost


def _cumsum_rows(tril_b, v):
    hi = v.astype(BF16)
    r1 = v - hi.astype(F32)
    mid = r1.astype(BF16)
    lo = (r1 - mid.astype(F32)).astype(BF16)
    w = v.shape[1]
    s = _dot(tril_b, jnp.concatenate([hi, mid, lo], axis=1))
    return s[:, :w] + s[:, w:2 * w] + s[:, 2 * w:]


def _ssd_items(slot, next_slot, convw_ref, convb_ref, alog_ref, dskip_ref, snw_ref, wbs_ref,
               ms_ref, yn_ref, state_ref):
    xraw_ref, dt_ref, zs_ref, gss_ref = slot
    tl = dt_ref.shape[0]
    tail = SUBLANES

    def conv_tile(ct, r0):
        cs = slice(ct * LANES, (ct + 1) * LANES)
        acc = convb_ref[:, cs]
        for k in range(SSD_CONV):
            off = r0 + tail - (SSD_CONV - 1) + k
            acc = acc + convw_ref[k:k + 1, cs] * xraw_ref[ct, off:off + Q, :]
        return _silu(acc)

    a_row = -jnp.exp(alog_ref[...])
    ri = lax.broadcasted_iota(jnp.int32, (Q, Q), 0)
    ci = lax.broadcasted_iota(jnp.int32, (Q, Q), 1)
    causal = ci <= ri
    tril_b = causal.astype(BF16)
    lane_lo = lax.broadcasted_iota(jnp.int32, (Q, LANES), 1) < SSD_HEADDIM
    b_tile = SSD_WIDTH // LANES
    c_tile = (SSD_WIDTH + SSD_BC) // LANES

    for c in range(tl // Q):
        r0 = c * Q
        rows = slice(r0, r0 + Q)
        dt = dt_ref[rows, :]
        a_cum = _cumsum_rows(tril_b, dt * a_row) * LOG2_E
        a_cum_t = a_cum.T
        row_t = a_cum_t - jnp.log2(dt.T)
        ea_all = jnp.exp2(a_cum)
        wcol_all = jnp.exp2(a_cum[Q - 1:Q, :] - a_cum) * dt
        yield 100
        for g in range(SSD_GROUPS):
            bg = conv_tile(b_tile + g, r0).astype(BF16)
            cg = conv_tile(c_tile + g, r0).astype(BF16)
            scores = lax.dot_general(cg, bg, (((1,), (1,)), ((), ())), preferred_element_type=F32)
            gcols = slice(g * SSD_GW, (g + 1) * SSD_GW)
            y_off = _dot(cg, state_ref[:, gcols].astype(BF16))
            yield 150
            xw_parts = []
            decay_parts = []
            y_parts = []
            for k in range(SSD_HPG // 2):
                cols = slice(g * SSD_GW + k * LANES, g * SSD_GW + (k + 1) * LANES)
                xs = conv_tile(cols.start // LANES, r0)
                wts, ea, wcol = [], [], []
                for h in (g * SSD_HPG + 2 * k, g * SSD_HPG + 2 * k + 1):
                    seg = jnp.broadcast_to(a_cum[:, h:h + 1], (Q, Q)) - row_t[h:h + 1, :]
                    lmat = jnp.exp2(jnp.where(causal, seg, -jnp.inf))
                    wts.append((scores * lmat).astype(BF16))
                    ea.append(jnp.broadcast_to(ea_all[:, h:h + 1], (Q, LANES)))
                    wcol.append(jnp.broadcast_to(wcol_all[:, h:h + 1], (Q, LANES)))
                xb = xs.astype(BF16)
                zero = jnp.zeros_like(xb)
                rhs = jnp.concatenate([jnp.where(lane_lo, xb, zero), jnp.where(lane_lo, zero, xb)], axis=0)
                y_diag = _dot(jnp.concatenate(wts, axis=1), rhs)
                ea2 = jnp.where(lane_lo, ea[0], ea[1])
                wcol2 = jnp.where(lane_lo, wcol[0], wcol[1])
                y_parts.append(y_diag + y_off[:, k * LANES:(k + 1) * LANES] * ea2 + xs * dskip_ref[:, cols])
                xw_parts.append((xs * wcol2).astype(BF16))
                decay_parts.append(ea2[Q - 1:Q, :])
                yield 250
            xw = jnp.concatenate(xw_parts, axis=1)
            new = lax.dot_general(bg, xw, (((0,), (0,)), ((), ())), preferred_element_type=F32)
            decay = jnp.concatenate(decay_parts, axis=1)
            state_ref[:, gcols] = state_ref[:, gcols] * decay + new
            yz = jnp.concatenate(y_parts, axis=1) * _silu(zs_ref[rows, gcols])
            yn_ref[rows, gcols] = _rms(yz, snw_ref[:, gcols]).astype(BF16)
            yield 250
    next_slot[0][:, 0:tail, :] = xraw_ref[:, tl:tl + tail, :]

    for c0 in range(0, D_MODEL, MXU_COLS):
        cs = slice(c0, c0 + MXU_COLS)
        ms_ref[:, cs] = (gss_ref[:, cs] * _dot(yn_ref[...], wbs_ref[:, cs])).astype(BF16)
        yield 400


def _mixer_in_kernel(x_ref, nw_ref, wmain_ref, wtail_ref,
                     convw_ref, convb_ref, dtb_ref, alog_ref, dskip_ref, snw_ref, wbs_ref,
                     u_ref, zs5_ref, g5_ref, mso_ref, mse_ref,
                     xraw0, dt0, zs0, gss0, xraw1, dt1, zs1, gss1, hb_ref, yn_ref, state_ref,
                     *, tiles_per_seq, last_step):
    m = pl.program_id(0)
    tl = dt0.shape[0]
    slots = ((xraw0, dt0, zs0, gss0), (xraw1, dt1, zs1, gss1))

    @pl.when(m == 0)
    def _():
        for ref in slots[0]:
            ref[...] = jnp.zeros(ref.shape, ref.dtype)
        state_ref[...] = jnp.zeros(state_ref.shape, F32)

    proj = functools.partial(_proj_items, x_ref, nw_ref, wmain_ref, wtail_ref, dtb_ref, u_ref, zs5_ref, g5_ref,
                             hb_ref)
    ssd = functools.partial(_ssd_items, convw_ref=convw_ref, convb_ref=convb_ref, alog_ref=alog_ref,
                            dskip_ref=dskip_ref, snw_ref=snw_ref, wbs_ref=wbs_ref,
                            yn_ref=yn_ref, state_ref=state_ref)

    _run_interleaved([ssd(slots[0], slots[1], ms_ref=mso_ref), proj(rows=slice(0, tl), slot=slots[1])])

    @pl.when(lax.rem(2 * m, tiles_per_seq) == 0)
    def _():
        xraw1[:, 0:SUBLANES, :] = jnp.zeros((xraw1.shape[0], SUBLANES, LANES), F32)
        state_ref[...] = jnp.zeros(state_ref.shape, F32)

    @pl.when(m < last_step)
    def _():
        _run_interleaved([ssd(slots[1], slots[0], ms_ref=mse_ref), proj(rows=slice(tl, 2 * tl), slot=slots[0])])


def _const_spec(shape):
    zeros = (0,) * len(shape)
    return pl.BlockSpec(shape, lambda *_: zeros)


def _mixer_in(x, norm_w, w_in, conv_w, conv_b, dt_bias, a_log, d_skip, ssd_norm_w, w_br_ssd):
    bsz, seq, d = x.shape
    tl = TL_IN
    assert seq % (2 * tl) == 0 and TL_OUT % (2 * tl) == 0 and seq % TL_OUT == 0
    rows_total = bsz * seq
    nsteps = rows_total // (2 * tl)
    n_main = 2 * S5_WIDTH + SSD_WIDTH + SSD_CONV_DIM
    pad = LANES - SSD_HEADS
    w_main = w_in.astype(BF16)
    w_tail = jnp.concatenate([w_main[:, n_main:n_main + SSD_HEADS], jnp.zeros((d, pad), BF16),
                              w_main[:, n_main + SSD_HEADS:]], axis=1)
    dtb = jnp.pad(dt_bias.reshape(1, SSD_HEADS), ((0, 0), (0, pad)))
    alog = jnp.pad(a_log.reshape(1, SSD_HEADS), ((0, 0), (0, pad)))
    dskip = jnp.repeat(d_skip, SSD_HEADDIM).reshape(1, SSD_WIDTH)
    consts = [norm_w.reshape(1, d), w_main, w_tail, conv_w, conv_b.reshape(1, -1), dtb, alog,
              dskip, ssd_norm_w.reshape(1, -1), w_br_ssd.astype(BF16)]
    spb = seq // (2 * tl)
    proj_spec = lambda w: pl.BlockSpec((2 * tl, w), lambda m: (jnp.minimum(m, nsteps - 1), 0))

    def seq_spec(w):
        def index(m):
            mm = jnp.minimum(m, nsteps - 1)
            return mm // spb, mm % spb, 0
        return pl.BlockSpec((None, 2 * tl, w), index)
    slot = [pltpu.VMEM((SSD_CONV_DIM // LANES, tl + SUBLANES, LANES), F32), pltpu.VMEM((tl, LANES), F32),
            pltpu.VMEM((tl, SSD_WIDTH), F32), pltpu.VMEM((tl, D_MODEL), F32)]
    return pl.pallas_call(
        functools.partial(_mixer_in_kernel, tiles_per_seq=seq // tl, last_step=nsteps),
        grid=(nsteps + 1,),
        in_specs=[seq_spec(d)] + [_const_spec(c.shape) for c in consts],
        out_specs=[seq_spec(S5_WIDTH), proj_spec(S5_WIDTH), proj_spec(D_MODEL),
                   pl.BlockSpec((tl, D_MODEL), lambda m: (jnp.maximum(m - 1, 0), 0)),
                   pl.BlockSpec((tl, D_MODEL), lambda m: (jnp.minimum(m, nsteps - 1), 0))],
        out_shape=[jax.ShapeDtypeStruct((bsz, seq, S5_WIDTH), F32),
                   jax.ShapeDtypeStruct((rows_total, S5_WIDTH), BF16),
                   jax.ShapeDtypeStruct((rows_total, D_MODEL), BF16),
                   jax.ShapeDtypeStruct((nsteps * tl, D_MODEL), BF16),
                   jax.ShapeDtypeStruct((nsteps * tl, D_MODEL), BF16)],
        scratch_shapes=slot + slot + [pltpu.VMEM((tl, D_MODEL), BF16),
                                      pltpu.VMEM((tl, SSD_WIDTH), BF16),
                                      pltpu.VMEM((SSD_STATE, SSD_WIDTH), F32)],
        compiler_params=pltpu.CompilerParams(dimension_semantics=("arbitrary",),
                                             vmem_limit_bytes=VMEM_LIMIT_BYTES),
        name="mixer_in",
    )(x, *consts)


def _s5_part_items(u_ref, bh_ref, ch_ref, lam_ref, d_ref, wglu_ref, bglu_ref, y_ref, st_ref, utm_ref, ytm_ref,
                   t0, steps):
    bsz = u_ref.shape[0]
    nlt = utm_ref.shape[0]
    r0 = t0 * bsz
    rows = slice(r0, r0 + steps * bsz)
    for b in range(bsz):
        for c in range(nlt):
            utm_ref[c, pl.ds(r0 + b, steps, stride=bsz), :] = u_ref[b, t0:t0 + steps, c * LANES:(c + 1) * LANES]
    yield 1
    u = jnp.concatenate([utm_ref[c, rows, :] for c in range(nlt)], axis=1)
    ub = u.astype(BF16)
    ys = []
    for q in range(S5_HALVES):
        ubq = ub[:, q * S5_HCH:(q + 1) * S5_HCH]
        yq = None
        for k in range(S5_HST // S5_CW):
            cols = slice(2 * k * S5_CW, 2 * (k + 1) * S5_CW)
            bu = _dot(ubq, bh_ref[q, :, cols])
            lam = lam_ref[q:q + 1, cols]
            lr = jnp.broadcast_to(lam[:, :S5_CW], (bsz, S5_CW))
            li = jnp.broadcast_to(lam[:, S5_CW:], (bsz, S5_CW))
            st = st_ref[q, :, cols]
            sr, si = st[:, :S5_CW], st[:, S5_CW:]
            hist = []
            for t in range(steps):
                r = slice(t * bsz, (t + 1) * bsz)
                sr, si = (lr * sr - li * si + bu[r, :S5_CW],
                          lr * si + li * sr + bu[r, S5_CW:])
                hist.append(jnp.concatenate([sr, si], axis=1).astype(BF16))
            st_ref[q, :, cols] = jnp.concatenate([sr, si], axis=1)
            part = _dot(jnp.concatenate(hist, axis=0), ch_ref[q, cols, :])
            yq = part if yq is None else yq + part
            yield 1
        ys.append(yq)
    y = jax.nn.gelu(jnp.concatenate(ys, axis=1) + d_ref[...] * u)
    yield 1
    y = y * _sigmoid(_dot(y.astype(BF16), wglu_ref[...]) + bglu_ref[...])
    for c in range(nlt):
        ytm_ref[c, rows, :] = y[:, c * LANES:(c + 1) * LANES]
    yield 1
    for b in range(bsz):
        for c in range(nlt):
            y_ref[b, t0:t0 + steps, c * LANES:(c + 1) * LANES] = (
                ytm_ref[c, pl.ds(r0 + b, steps, stride=bsz), :].astype(BF16))
    yield 1


def _delayed(stream, n):
    for _ in range(n):
        yield 1
    yield from stream


def _s5_scan_kernel(u_ref, bh_ref, ch_ref, lam_ref, d_ref, wglu_ref, bglu_ref, y_ref,
                    st_ref, utm_ref, ytm_ref):
    @pl.when(pl.program_id(0) == 0)
    def _():
        st_ref[...] = jnp.zeros(st_ref.shape, F32)

    part = u_ref.shape[1] // S5_PARTS
    _run_interleaved([
        _delayed(_s5_part_items(u_ref, bh_ref, ch_ref, lam_ref, d_ref, wglu_ref, bglu_ref, y_ref, st_ref,
                                utm_ref, ytm_ref, j * part, part), j * S5_LAG)
        for j in range(S5_PARTS)])


def _s5_scan(u, bh, ch, lam, s5_d, w_glu, b_glu):
    bsz, seq, _ = u.shape
    steps = TT_S5
    rows = steps * bsz
    consts = [bh, ch, lam, s5_d.reshape(1, -1), w_glu.astype(BF16), b_glu.reshape(1, -1)]
    return pl.pallas_call(
        _s5_scan_kernel,
        grid=(seq // steps,),
        in_specs=[pl.BlockSpec((bsz, steps, S5_WIDTH), lambda i: (0, i, 0))]
                 + [_const_spec(c.shape) for c in consts],
        out_specs=pl.BlockSpec((bsz, steps, S5_WIDTH), lambda i: (0, i, 0)),
        out_shape=jax.ShapeDtypeStruct((bsz, seq, S5_WIDTH), BF16),
        scratch_shapes=[pltpu.VMEM((S5_HALVES, bsz, 2 * S5_HST), F32),
                        pltpu.VMEM((S5_WIDTH // LANES, rows, LANES), F32),
                        pltpu.VMEM((S5_WIDTH // LANES, rows, LANES), F32)],
        compiler_params=pltpu.CompilerParams(dimension_semantics=("arbitrary",),
                                             vmem_limit_bytes=VMEM_LIMIT_BYTES),
        name="s5_scan",
    )(u, *consts)


def _merge_rows(x_ref, p_ref, y5_ref, zs5_ref, g5_ref, ms_ref, wb5_ref, wout_ref, pnw_ref, wpg_ref, wpp_ref,
                fnw_ref, o_ref, rows, ms_rows):
    y5 = (y5_ref[rows, :].astype(F32) * zs5_ref[rows, :].astype(F32)).astype(BF16)
    yield 1
    branch = _dot(y5, wb5_ref[...])
    yield 1
    merged = (g5_ref[rows, :].astype(F32) * branch + ms_ref[ms_rows, :].astype(F32)).astype(BF16)
    yield 1
    h = x_ref[rows, :] + _dot(merged, wout_ref[...])
    yield 1
    hn = _rms(h, pnw_ref[...]).astype(BF16)
    yield 1
    gate = _sigmoid(_dot(hn, wpg_ref[...]))
    yield 1
    h = h + gate * _dot(p_ref[rows, :].astype(BF16), wpp_ref[...])
    yield 1
    o_ref[rows, :] = _rms(h, fnw_ref[...])
    yield 1


def _merge_out_kernel(x_ref, p_ref, y5_ref, zs5_ref, g5_ref, mse_ref, mso_ref, wb5_ref, wout_ref, pnw_ref,
                      wpg_ref, wpp_ref, fnw_ref, o_ref):
    sub = TL_IN
    _run_interleaved([
        _delayed(_merge_rows(x_ref, p_ref, y5_ref, zs5_ref, g5_ref, (mse_ref, mso_ref)[r % 2], wb5_ref, wout_ref,
                             pnw_ref, wpg_ref, wpp_ref, fnw_ref, o_ref, slice(r * sub, (r + 1) * sub),
                             slice((r // 2) * sub, (r // 2 + 1) * sub)), r * MERGE_LAG)
        for r in range(x_ref.shape[0] // sub)])


def _merge_out(x, p, y5, zs5, g5, ms_even, ms_odd, w_br_s5, w_out, ple_norm_w, w_ple_gate, w_ple_proj,
               final_norm_w):
    bsz, seq, d = x.shape
    tl = TL_OUT
    consts = [w_br_s5.astype(BF16), w_out.astype(BF16), ple_norm_w.reshape(1, d), w_ple_gate.astype(BF16),
              w_ple_proj.astype(BF16), final_norm_w.reshape(1, d)]
    tok_spec = lambda w, rows=tl: pl.BlockSpec((rows, w), lambda b, t: (b * (seq // tl) + t, 0))
    return pl.pallas_call(
        _merge_out_kernel,
        grid=(bsz, seq // tl),
        in_specs=[pl.BlockSpec((None, tl, d), lambda b, t: (b, t, 0)),
                  pl.BlockSpec((None, tl, PLE_DIM), lambda b, t: (b, t, 0)),
                  pl.BlockSpec((None, tl, S5_WIDTH), lambda b, t: (b, t, 0)),
                  tok_spec(S5_WIDTH), tok_spec(D_MODEL),
                  tok_spec(D_MODEL, tl // 2), tok_spec(D_MODEL, tl // 2)]
                 + [_const_spec(c.shape) for c in consts],
        out_specs=pl.BlockSpec((None, tl, d), lambda b, t: (b, t, 0)),
        out_shape=jax.ShapeDtypeStruct((bsz, seq, d), F32),
        compiler_params=pltpu.CompilerParams(dimension_semantics=("parallel", "parallel"),
                                             vmem_limit_bytes=VMEM_LIMIT_BYTES),
        name="merge_out",
    )(x, p, y5, zs5, g5, ms_even, ms_odd, *consts)


def kernel(x, p, norm_w, w_in, s5_a_re, s5_a_im, s5_b_re, s5_b_im, s5_c_re, s5_c_im, s5_d, s5_log_step, s5_w_glu, s5_b_glu, ssd_conv_w, ssd_conv_b, ssd_dt_bias, ssd_a_log, ssd_d, ssd_norm_w, w_br_s5, w_br_ssd, w_out, ple_norm_w, w_ple_gate, w_ple_proj, final_norm_w):
    bsz, seq, _ = x.shape
    i = 0
    assert norm_w.shape[0] == 1

    lam, bh, ch = _s5_discretize(s5_a_re[i], s5_a_im[i], s5_log_step[i], s5_b_re[i], s5_b_im[i],
                                 s5_c_re[i], s5_c_im[i])

    u, zs5, g5, ms_odd, ms_even = _mixer_in(x, norm_w[i], w_in[i], ssd_conv_w[i], ssd_conv_b[i],
                                            ssd_dt_bias[i], ssd_a_log[i], ssd_d[i], ssd_norm_w[i], w_br_ssd[i])
    y5 = _s5_scan(u, bh, ch, lam, s5_d[i], s5_w_glu[i], s5_b_glu[i])
    return _merge_out(x, p[i], y5, zs5, g5, ms_even, ms_odd, w_br_s5[i], w_out[i], ple_norm_w[i],
                      w_ple_gate[i], w_ple_proj[i], final_norm_w)
```
